```python
import math
import jax, jax.numpy as jnp
from jax import lax
import numpy as np

D_MODEL = 2048
BATCH = 32
SEQ = 256
DEPTH = 2
DEC_BATCH = 4
DEC_SEQ = 1024
PAST_LEN = 512

GRID_W = 64
BLOCK = 128
EPS = 1e-6
NEG = -1e30
N_ATTN_LAYERS = (DEPTH + 1) // 2
N_SSD_LAYERS = DEPTH // 2

HD = 128
A_HEADS = 8
A_KV_HEADS = 2
A_GROUP = A_HEADS // A_KV_HEADS
WINDOW = 128
B_HEADS = 4
B_VD = 2 * HD
A_Q = A_HEADS * HD
A_KV = A_KV_HEADS * HD
B_QK = B_HEADS * 2 * HD
B_V = B_HEADS * B_VD
ATTN_IN = A_Q + 2 * A_KV + 2 * B_QK + B_V
ATTN_MIX = A_Q + B_V
ATTN_SPLITS = (A_Q, A_Q + A_KV, A_Q + 2 * A_KV, A_Q + 2 * A_KV + B_QK, A_Q + 2 * A_KV + 2 * B_QK)
ROPE_BASE = 10000.0

D_INNER = 2 * D_MODEL
SSD_HEADDIM = 64
SSD_HEADS = D_INNER // SSD_HEADDIM
SSD_GROUPS = 8
SSD_HPG = SSD_HEADS // SSD_GROUPS
SSD_STATE = 128
CONV_K = 5
CONV_CH = D_INNER + 2 * SSD_GROUPS * SSD_STATE
SSD_IN = D_INNER + CONV_CH + 2 * SSD_HEADS
CHUNK = 128

PEER_HEADS = 8
PEER_KEYS = 128
PEER_EXPERTS = PEER_KEYS * PEER_KEYS
PEER_TOPK = 16
PEER_DKEY = 256
PEER_DHALF = PEER_DKEY // 2
PEER_BLOCK = 128

kernel_name = 'hybrid_diffusion_prefix_ctx_step'


def rmsnorm(x, g):
    xf = x.astype(jnp.float32)
    y = xf * lax.rsqrt(jnp.mean(xf * xf, axis=-1, keepdims=True) + EPS)
    return (y * g.astype(jnp.float32)).astype(x.dtype)


def modulation(cond, w, b):
    return jnp.split(jax.nn.silu(cond) @ w + b, 6, axis=-1)


def adaln(x, g, shift, scale):
    return rmsnorm(x, g) * (1.0 + scale[:, None]) + shift[:, None]


def axial_rope(L, dtype):
    rows = L // GRID_W
    row = jnp.repeat(jnp.arange(rows), GRID_W).astype(jnp.float32)
    col = jnp.tile(jnp.arange(GRID_W), rows).astype(jnp.float32)
    half = HD // 2
    inv = ROPE_BASE ** (-jnp.arange(0, half, 2, dtype=jnp.float32) / half)
    ar = row[:, None] * inv
    ac = col[:, None] * inv
    ang = jnp.concatenate([ar, ar, ac, ac], axis=-1)
    return jnp.cos(ang).astype(dtype), jnp.sin(ang).astype(dtype)


def apply_rope(x, cos, sin):
    xr = x.reshape(x.shape[:-1] + (2, 2, HD // 4))
    rot = jnp.stack([-xr[..., 1, :], xr[..., 0, :]], axis=-2).reshape(x.shape)
    shp = (1, x.shape[1]) + (1,) * (x.ndim - 3) + (HD,)
    return x * cos.reshape(shp) + rot * sin.reshape(shp)


def attend(q, segs, sink=None):
    scale = q.shape[-1] ** -0.5
    logits = []
    for k, v, m in segs:
        s = jnp.einsum('bqhgd,bkhd->bhgqk', q, k, preferred_element_type=jnp.float32) * scale
        if m is not None:
            s = jnp.where(m, s, NEG)
        logits.append(s)
    if sink is not None:
        B, Tq = q.shape[:2]
        logits.append(jnp.broadcast_to(sink.astype(jnp.float32)[None, :, :, None, None], (B,) + sink.shape + (Tq, 1)))
    p = jax.nn.softmax(jnp.concatenate(logits, axis=-1), axis=-1)
    out = 0
    off = 0
    for k, v, _ in segs:
        tk = k.shape[1]
        out = out + jnp.einsum('bhgqk,bkhv->bqhgv', p[..., off:off + tk].astype(v.dtype), v)
        off += tk
    return out


def to_blocks(x):
    B, L = x.shape[:2]
    return jnp.moveaxis(x.reshape((B, L // BLOCK, BLOCK) + x.shape[2:]), 1, 0)


def from_blocks(x):
    nb, B = x.shape[:2]
    return jnp.moveaxis(x, 0, 1).reshape((B, nb * BLOCK) + x.shape[3:])


def band_windows(x):
    B, L = x.shape[:2]
    nb = L // BLOCK
    pad = [(0, 0), (BLOCK, BLOCK)] + [(0, 0)] * (x.ndim - 2)
    xb = jnp.pad(x, pad).reshape((B, nb + 2, BLOCK) + x.shape[2:])
    w = jnp.concatenate([xb[:, :-2], xb[:, 1:-1], xb[:, 2:]], axis=2)
    return jnp.moveaxis(w, 1, 0)


def split_attn_proj(p):
    B, L = p.shape[:2]
    aq, ak, av, bq, bk, bv = jnp.split(p, ATTN_SPLITS, axis=-1)
    return (aq.reshape(B, L, A_KV_HEADS, A_GROUP, HD), ak.reshape(B, L, A_KV_HEADS, HD),
            av.reshape(B, L, A_KV_HEADS, HD), bq.reshape(B, L, B_HEADS, 2, HD),
            bk.reshape(B, L, B_HEADS, 2, HD), bv.reshape(B, L, B_HEADS, B_VD))


def diff_lambda_init(layer):
    return 0.8 - 0.6 * math.exp(-0.3 * layer)


def diff_lambda(lam_vecs, lam_init):
    lv = lam_vecs.astype(jnp.float32)
    return jnp.exp(jnp.sum(lv[0] * lv[1])) - jnp.exp(jnp.sum(lv[2] * lv[3])) + lam_init


def attn_out(oa, ob, subln_g, lam_init, w_out):
    B, L = oa.shape[:2]
    ob = rmsnorm(ob.reshape(B, L, B_HEADS, B_VD), subln_g) * (1.0 - lam_init)
    mixed = jnp.concatenate([oa.reshape(B, L, A_Q), ob.reshape(B, L, B_V)], axis=-1)
    return mixed @ w_out


def attn_mix_context(h, w_in, w_out, sink, lam_vecs, subln_g, lam_init):
    aq, ak, av, bq, bk, bv = split_attn_proj(h @ w_in)
    lam = diff_lambda(lam_vecs, lam_init).astype(h.dtype)
    k1, k2 = bk[:, :, :, 0], bk[:, :, :, 1]

    def blk_a(qb):
        return attend(qb, [(ak, av, None)], sink)

    def blk_b(qs):
        q1, q2 = qs
        return attend(q1, [(k1, bv, None)]) - lam * attend(q2, [(k2, bv, None)])

    oa = from_blocks(lax.map(blk_a, to_blocks(aq)))
    ob = from_blocks(lax.map(blk_b, (to_blocks(bq[:, :, :, 0:1]), to_blocks(bq[:, :, :, 1:2]))))
    return attn_out(oa, ob, subln_g, lam_init, w_out), (ak, av, bk, bv)


def attn_mix_latent(h, ck_a, cv_a, ck_b, cv_b, w_in, w_out, sink, lam_vecs, subln_g, lam_init):
    B, L = h.shape[:2]
    aq, ak, av, bq, bk, bv = split_attn_proj(h @ w_in)
    cos, sin = axial_rope(L, h.dtype)
    aq, ak, bq, bk = (apply_rope(t, cos, sin) for t in (aq, ak, bq, bk))
    lam = diff_lambda(lam_vecs, lam_init).astype(h.dtype)
    nb = L // BLOCK
    qpos = jnp.arange(BLOCK)[:, None]
    kpos = jnp.arange(3 * BLOCK)[None, :]
    rel = kpos - BLOCK - qpos

    def blk_a(args):
        qb, kb, vb, i = args
        s = i * BLOCK + kpos - BLOCK
        m = (jnp.abs(rel) <= WINDOW) & (s >= 0) & (s < L)
        return attend(qb, [(ck_a, cv_a, None), (kb, vb, m)], sink)

    ck1, ck2 = ck_b[:, :, :, 0], ck_b[:, :, :, 1]
    k1, k2 = bk[:, :, :, 0], bk[:, :, :, 1]

    def blk_b(qs):
        q1, q2 = qs
        return (attend(q1, [(ck1, cv_b, None), (k1, bv, None)])
                - lam * attend(q2, [(ck2, cv_b, None), (k2, bv, None)]))

    oa = from_blocks(lax.map(blk_a, (to_blocks(aq), band_windows(ak), band_windows(av), jnp.arange(nb))))
    ob = from_blocks(lax.map(blk_b, (to_blocks(bq[:, :, :, 0:1]), to_blocks(bq[:, :, :, 1:2]))))
    return attn_out(oa, ob, subln_g, lam_init, w_out)


def dwconv(x, w, b):
    y = lax.conv_general_dilated(x, w[:, None, :], window_strides=(1,), padding=[(CONV_K // 2, CONV_K // 2)],
                                 dimension_numbers=('NWC', 'WIO', 'NWC'), feature_group_count=x.shape[-1])
    return y + b


def ssd_scan(x, dt, a, bm, cm, h0):
    f32 = jnp.float32
    B, L = x.shape[:2]
    nc = L // CHUNK
    cshape = (B, nc, CHUNK, SSD_GROUPS, SSD_HPG)
    dtc = dt.reshape(cshape)
    ac = jnp.cumsum(dtc * a.reshape(SSD_GROUPS, SSD_HPG), axis=2)
    dtx = x.astype(f32).reshape(cshape + (SSD_HEADDIM,)) * dtc[..., None]
    bc = bm.astype(f32).reshape(B, nc, CHUNK, SSD_GROUPS, SSD_STATE)
    cc = cm.astype(f32).reshape(B, nc, CHUNK, SSD_GROUPS, SSD_STATE)
    causal = jnp.tril(jnp.ones((CHUNK, CHUNK), bool))[:, :, None, None]
    decay_in = jnp.exp(jnp.where(causal, ac[:, :, :, None] - ac[:, :, None, :], NEG))
    scores = jnp.einsum('bclgn,bcsgn->bclsg', cc, bc)
    y_diag = jnp.einsum('bclsgh,bcsghp->bclghp', scores[..., None] * decay_in, dtx)
    decay_out = jnp.exp(ac[:, :, -1:] - ac)
    states = jnp.einsum('bclgn,bclghp->bcghpn', bc, dtx * decay_out[..., None])
    chunk_decay = jnp.exp(ac[:, :, -1])

    def step(hs, inp):
        st, dec = inp
        return dec[..., None, None] * hs + st, hs

    h_init = h0.astype(f32).reshape(B, SSD_GROUPS, SSD_HPG, SSD_HEADDIM, SSD_STATE)
    h_last, h_prev = lax.scan(step, h_init, (jnp.moveaxis(states, 1, 0), jnp.moveaxis(chunk_decay, 1, 0)))
    h_in = jnp.moveaxis(h_prev, 0, 1)
    y_off = jnp.einsum('bclgn,bcghpn->bclghp', cc, h_in) * jnp.exp(ac)[..., None]
    y = (y_diag + y_off).reshape(B, L, SSD_HEADS, SSD_HEADDIM)
    return y.astype(x.dtype), h_last.reshape(B, SSD_HEADS, SSD_HEADDIM, SSD_STATE).astype(x.dtype)


def ssd_mix(h, h0_fwd, h0_bwd, w_in, conv_w, conv_b, dt_bias, a_log, d_skip, norm_g, w_out):
    B, L = h.shape[:2]
    z, xbc, dt = jnp.split(h @ w_in, (D_INNER, D_INNER + CONV_CH), axis=-1)
    xbc = jax.nn.silu(dwconv(xbc, conv_w, conv_b))
    xs, bm, cm = jnp.split(xbc, (D_INNER, D_INNER + SSD_GROUPS * SSD_STATE), axis=-1)
    xs = xs.reshape(B, L, SSD_HEADS, SSD_HEADDIM)
    bm = bm.reshape(B, L, SSD_GROUPS, SSD_STATE)
    cm = cm.reshape(B, L, SSD_GROUPS, SSD_STATE)
    dt = jax.nn.softplus(dt.astype(jnp.float32).reshape(B, L, 2, SSD_HEADS) + dt_bias.astype(jnp.float32))
    a = -jnp.exp(a_log.astype(jnp.float32))
    y_f, h_f = ssd_scan(xs, dt[:, :, 0], a[0], bm, cm, h0_fwd)
    flip = lambda t: jnp.flip(t, axis=1)
    y_b, h_b = ssd_scan(flip(xs), flip(dt[:, :, 1]), a[1], flip(bm), flip(cm), h0_bwd)
    y = y_f + flip(y_b) + d_skip[:, None] * xs
    gs = D_INNER // SSD_GROUPS
    y = (y.reshape(B, L, D_INNER) * jax.nn.silu(z)).reshape(B, L, SSD_GROUPS, gs)
    y = rmsnorm(y, norm_g.reshape(SSD_GROUPS, gs)).reshape(B, L, D_INNER)
    return y @ w_out, (h_f, h_b)


def peer(h, w_q, sub_keys, u, v):
    B, L, D = h.shape

    def blk(xb):
        q = (xb @ w_q).reshape(PEER_BLOCK, PEER_HEADS, 2, PEER_DHALF)
        s = jnp.einsum('thcd,hckd->thck', q, sub_keys, preferred_element_type=jnp.float32)
        sv, si = lax.top_k(s, PEER_TOPK)
        cand = (sv[:, :, 0, :, None] + sv[:, :, 1, None, :]).reshape(PEER_BLOCK, PEER_HEADS, PEER_TOPK * PEER_TOPK)
        cv, ci = lax.top_k(cand, PEER_TOPK)
        i1 = jnp.take_along_axis(si[:, :, 0], ci // PEER_TOPK, axis=-1)
        i2 = jnp.take_along_axis(si[:, :, 1], ci % PEER_TOPK, axis=-1)
        idx = i1 * PEER_KEYS + i2
        g = jax.nn.softmax(cv, axis=-1)
        act = jax.nn.gelu(jnp.einsum('td,thkd->thk', xb, u[idx]))
        return jnp.einsum('thk,thkd->td', (g * act).astype(xb.dtype), v[idx])

    return lax.map(blk, h.reshape(-1, PEER_BLOCK, D)).reshape(B, L, D)


def setup_inputs(seed: int = 0) -> dict:
    key = jax.random.key(seed)
    ks = iter(jax.random.split(key, 48))

    def nrm(shape, scale=1.0):
        return jax.random.normal(next(ks), shape, jnp.float32) * scale

    NA, NS = N_ATTN_LAYERS, N_SSD_LAYERS
    dt0 = jnp.exp(jax.random.uniform(next(ks), (NS, 2, SSD_HEADS), jnp.float32, math.log(1e-3), math.log(1e-1)))
    a0 = jax.random.uniform(next(ks), (NS, 2, SSD_HEADS), jnp.float32, 1.0, 16.0)
    return {
        'x_prompt': nrm((BATCH, SEQ, D_MODEL)),
        'x_sample': nrm((DEC_BATCH, DEC_SEQ, D_MODEL)),
        'cache_a_k': nrm((DEC_BATCH, NA, PAST_LEN, A_KV_HEADS, HD)),
        'cache_a_v': nrm((DEC_BATCH, NA, PAST_LEN, A_KV_HEADS, HD)),
        'cache_b_k': nrm((DEC_BATCH, NA, PAST_LEN, B_HEADS, 2, HD)),
        'cache_b_v': nrm((DEC_BATCH, NA, PAST_LEN, B_HEADS, B_VD)),
        'state_ssd_fwd': nrm((DEC_BATCH, NS, SSD_HEADS, SSD_HEADDIM, SSD_STATE), 0.1),
        'state_ssd_bwd': nrm((DEC_BATCH, NS, SSD_HEADS, SSD_HEADDIM, SSD_STATE), 0.1),
        'c': nrm((DEC_BATCH, D_MODEL)),
        'c_ctx': nrm((D_MODEL,)),
        'w_mod': nrm((DEPTH, D_MODEL, 6 * D_MODEL), D_MODEL ** -0.5),
        'b_mod': nrm((DEPTH, 6 * D_MODEL), 0.02),
        'norm_g': 1.0 + nrm((DEPTH, 2, D_MODEL), 0.02),
        'attn_w_in': nrm((NA, D_MODEL, ATTN_IN), D_MODEL ** -0.5),
        'attn_w_out': nrm((NA, ATTN_MIX, D_MODEL), ATTN_MIX ** -0.5),
        'attn_sink': nrm((NA, A_KV_HEADS, A_GROUP), 0.5),
        'diff_lam': nrm((NA, 4, HD), 0.1),
        'diff_subln_g': 1.0 + nrm((NA, B_VD), 0.02),
        'ssd_w_in': nrm((NS, D_MODEL, SSD_IN), D_MODEL ** -0.5),
        'ssd_conv_w': nrm((NS, CONV_K, CONV_CH), CONV_K ** -0.5),
        'ssd_conv_b': nrm((NS, CONV_CH), 0.02),
        'ssd_dt_bias': dt0 + jnp.log(-jnp.expm1(-dt0)),
        'ssd_a_log': jnp.log(a0),
        'ssd_d': 1.0 + nrm((NS, SSD_HEADS), 0.02),
        'ssd_norm_g': 1.0 + nrm((NS, D_INNER), 0.02),
        'ssd_w_out': nrm((NS, D_INNER, D_MODEL), D_INNER ** -0.5),
        'peer_w_q': nrm((DEPTH, D_MODEL, PEER_HEADS * PEER_DKEY), D_MODEL ** -0.5),
        'peer_sub_keys': nrm((DEPTH, PEER_HEADS, 2, PEER_KEYS, PEER_DHALF), PEER_DHALF ** -0.5),
        'peer_u': nrm((DEPTH, PEER_EXPERTS, D_MODEL), D_MODEL ** -0.5),
        'peer_v': nrm((DEPTH, PEER_EXPERTS, D_MODEL), (PEER_HEADS * PEER_TOPK) ** -0.5),
        'final_g': 1.0 + nrm((D_MODEL,), 0.02),
    }


def reference(x_prompt, x_sample, cache_a_k, cache_a_v, cache_b_k, cache_b_v, state_ssd_fwd, state_ssd_bwd,
              c, c_ctx, w_mod, b_mod, norm_g, attn_w_in, attn_w_out, attn_sink, diff_lam, diff_subln_g,
              ssd_w_in, ssd_conv_w, ssd_conv_b, ssd_dt_bias, ssd_a_log, ssd_d, ssd_norm_g, ssd_w_out,
              peer_w_q, peer_sub_keys, peer_u, peer_v, final_g):
    x = x_prompt
    Bp = x.shape[0]
    ak_l, av_l, bk_l, bv_l, sf_l, sb_l = [], [], [], [], [], []
    for l in range(DEPTH):
        i = l // 2
        sh1, sc1, g1, sh2, sc2, g2 = modulation(c_ctx[None], w_mod[l], b_mod[l])
        h = adaln(x, norm_g[l, 0], sh1, sc1)
        if l % 2 == 0:
            out, (ak, av, bk, bv) = attn_mix_context(h, attn_w_in[i], attn_w_out[i], attn_sink[i], diff_lam[i],
                                                     diff_subln_g[i], diff_lambda_init(l))
            ak_l.append(ak); av_l.append(av); bk_l.append(bk); bv_l.append(bv)
        else:
            h0 = jnp.zeros((Bp, SSD_HEADS, SSD_HEADDIM, SSD_STATE), jnp.float32)
            out, (sf, sb) = ssd_mix(h, h0, h0, ssd_w_in[i], ssd_conv_w[i], ssd_conv_b[i], ssd_dt_bias[i],
                                    ssd_a_log[i], ssd_d[i], ssd_norm_g[i], ssd_w_out[i])
            sf_l.append(sf); sb_l.append(sb)
        x = x + g1[:, None] * out
        x = x + g2[:, None] * peer(adaln(x, norm_g[l, 1], sh2, sc2), peer_w_q[l], peer_sub_keys[l], peer_u[l], peer_v[l])
    y_prompt = rmsnorm(x, final_g)

    x = x_sample
    for l in range(DEPTH):
        i = l // 2
        sh1, sc1, g1, sh2, sc2, g2 = modulation(c, w_mod[l], b_mod[l])
        h = adaln(x, norm_g[l, 0], sh1, sc1)
        if l % 2 == 0:
            out = attn_mix_latent(h, cache_a_k[:, i], cache_a_v[:, i], cache_b_k[:, i], cache_b_v[:, i],
                                  attn_w_in[i], attn_w_out[i], attn_sink[i], diff_lam[i], diff_subln_g[i],
                                  diff_lambda_init(l))
        else:
            out, _ = ssd_mix(h, state_ssd_fwd[:, i], state_ssd_bwd[:, i], ssd_w_in[i], ssd_conv_w[i], ssd_conv_b[i],
                             ssd_dt_bias[i], ssd_a_log[i], ssd_d[i], ssd_norm_g[i], ssd_w_out[i])
        x = x + g1[:, None] * out
        x = x + g2[:, None] * peer(adaln(x, norm_g[l, 1], sh2, sc2), peer_w_q[l], peer_sub_keys[l], peer_u[l], peer_v[l])
    y_sample = rmsnorm(x, final_g)

    new_a_k = jnp.stack(ak_l, axis=1)
    new_a_v = jnp.stack(av_l, axis=1)
    new_b_k = jnp.stack(bk_l, axis=1)
    new_b_v = jnp.stack(bv_l, axis=1)
    new_ssd_fwd = jnp.stack(sf_l, axis=1)
    new_ssd_bwd = jnp.stack(sb_l, axis=1)
    return (y_prompt, y_sample, new_a_k, new_a_v, new_b_k, new_b_v, new_ssd_fwd, new_ssd_bwd)
```

```python
import functools
import math

import jax
import jax.numpy as jnp
from jax import lax
from jax.experimental import pallas as pl
from jax.experimental.pallas import tpu as pltpu

F32 = jnp.float32
BF16 = jnp.bfloat16

D_MODEL = 2048
DEPTH = 2
GRID_W = 64
EPS = 1e-6
NEG = -1e30

HD = 128
A_HEADS = 8
A_KV_HEADS = 2
A_GROUP = A_HEADS // A_KV_HEADS
WINDOW = 128
B_HEADS = 4
B_VD = 2 * HD
A_Q = A_HEADS * HD
A_KV = A_KV_HEADS * HD
B_QK = B_HEADS * 2 * HD
B_V = B_HEADS * B_VD
ATTN_IN = A_Q + 2 * A_KV + 2 * B_QK + B_V
ATTN_MIX = A_Q + B_V
ROPE_BASE = 10000.0

D_INNER = 2 * D_MODEL
SSD_HEADDIM = 64
SSD_HEADS = D_INNER // SSD_HEADDIM
SSD_GROUPS = 8
SSD_HPG = SSD_HEADS // SSD_GROUPS
SSD_STATE = 128
CONV_K = 5
CONV_CH = D_INNER + 2 * SSD_GROUPS * SSD_STATE
CHUNK = 128

PEER_HEADS = 8
PEER_KEYS = 128
PEER_EXPERTS = PEER_KEYS * PEER_KEYS
PEER_TOPK = 16
PEER_DKEY = 256
PEER_DHALF = PEER_DKEY // 2

V7X_VMEM_LIMIT_BYTES = 56 * 1024 * 1024
LANES = 128
SUBLANES = 8

HIGHEST = lax.Precision.HIGHEST


def _cparams(*sem):
    return pltpu.CompilerParams(dimension_semantics=sem, vmem_limit_bytes=V7X_VMEM_LIMIT_BYTES)


def _dot(a, b, precision=None):
    return jnp.dot(a, b, preferred_element_type=F32, precision=precision)


def _dot_nt(a, b):
    return lax.dot_general(a, b, (((1,), (1,)), ((), ())), preferred_element_type=F32)


def _dot_tn(a, b):
    return lax.dot_general(a, b, (((0,), (0,)), ((), ())), preferred_element_type=F32)


def _sigmoid(x):
    return 1.0 / (1.0 + jnp.exp(-x))


def _cond_of_tile(i, tm, n_ctx, ld):
    r = i * tm
    return jnp.where(r < n_ctx, 0, 1 + (r - n_ctx) // ld)


def _mod_kernel(c_ref, w_ref, b_ref, o_ref):
    c = c_ref[...]
    s = c * _sigmoid(c)
    o_ref[...] = _dot(s.astype(BF16), w_ref[...].astype(BF16)) + b_ref[...]


def _modulation(cond, w_mod, b_mod):
    depth, d, n6 = w_mod.shape
    r = cond.shape[0]
    tn = 1024
    return pl.pallas_call(
        _mod_kernel,
        grid=(depth, n6 // tn),
        in_specs=[
            pl.BlockSpec((r, d), lambda l, j: (0, 0)),
            pl.BlockSpec((None, d, tn), lambda l, j: (l, 0, j)),
            pl.BlockSpec((None, 1, tn), lambda l, j: (l, 0, j)),
        ],
        out_specs=pl.BlockSpec((None, r, tn), lambda l, j: (l, 0, j)),
        out_shape=jax.ShapeDtypeStruct((depth, r, n6), F32),
        compiler_params=_cparams("parallel", "parallel"),
    )(cond, w_mod, b_mod.reshape(depth, 1, n6))


def _proj_kernel(*refs, adaln, emit_h, residual):
    it = iter(refs)
    x_ref = next(it)
    if adaln:
        g_ref, sh_ref, sc_ref = next(it), next(it), next(it)
    w_ref = next(it)
    if residual:
        r_ref, gate_ref = next(it), next(it)
    o_ref = next(it)
    if emit_h:
        h_ref = next(it)
    if adaln:
        hs_ref = next(it)
        j = pl.program_id(1)

        @pl.when(j == 0)
        def _():
            x = x_ref[...]
            y = x * lax.rsqrt(jnp.mean(x * x, axis=-1, keepdims=True) + EPS)
            y = y * g_ref[...]
            y = y * (1.0 + sc_ref[0]) + sh_ref[0]
            hb = y.astype(BF16)
            hs_ref[...] = hb
            if emit_h:
                h_ref[...] = hb

        h = hs_ref[...]
    else:
        h = x_ref[...]
    acc = _dot(h, w_ref[...])
    if residual:
        o_ref[...] = r_ref[...] + gate_ref[0] * acc
    else:
        o_ref[...] = acc


def _proj(x, w, *, tm, tn, n_ctx, ld, mod=None, norm_g=None, shift_k=None, scale_k=None,
          emit_h=False, res=None, gate_k=None):
    n, k = x.shape
    m = w.shape[1]
    adaln = norm_g is not None
    residual = res is not None
    cond = functools.partial(_cond_of_tile, tm=tm, n_ctx=n_ctx, ld=ld)
    in_specs = [pl.BlockSpec((tm, k), lambda i, j: (i, 0))]
    args = [x]
    if adaln:
        in_specs += [
            pl.BlockSpec((1, k), lambda i, j: (0, 0)),
            pl.BlockSpec((1, 1, k), lambda i, j: (cond(i) * 6 + shift_k, 0, 0)),
            pl.BlockSpec((1, 1, k), lambda i, j: (cond(i) * 6 + scale_k, 0, 0)),
        ]
        args += [norm_g.reshape(1, k), mod, mod]
    in_specs.append(pl.BlockSpec((k, tn), lambda i, j: (0, j)))
    args.append(w)
    if residual:
        in_specs += [
            pl.BlockSpec((tm, tn), lambda i, j: (i, j)),
            pl.BlockSpec((1, 1, tn), lambda i, j: (cond(i) * 6 + gate_k, 0, j)),
        ]
        args += [res, mod]
    out_specs = [pl.BlockSpec((tm, tn), lambda i, j: (i, j))]
    out_shape = [jax.ShapeDtypeStruct((n, m), F32)]
    if emit_h:
        out_specs.append(pl.BlockSpec((tm, k), lambda i, j: (i, 0)))
        out_shape.append(jax.ShapeDtypeStruct((n, k), BF16))
    scratch = [pltpu.VMEM((tm, k), BF16)] if adaln else []
    outs = pl.pallas_call(
        functools.partial(_proj_kernel, adaln=adaln, emit_h=emit_h, residual=residual),
        grid=(n // tm, m // tn),
        in_specs=in_specs,
        out_specs=out_specs,
        out_shape=out_shape,
        scratch_shapes=scratch,
        compiler_params=_cparams("parallel", "arbitrary"),
    )(*args)
    return outs if emit_h else outs[0]


def _attn_kernel(*refs, L, tq, n_cache, rope, window, lam_init):
    it = iter(refs)
    sink_ref = next(it)
    lam_ref = next(it)
    sub_ref = next(it)
    if rope:
        cos_ref, sa_ref, sb_ref = next(it), next(it), next(it)
    qa_ref, qb_ref, kb_ref, vb_ref, kva_ref = (next(it) for _ in range(5))
    if n_cache:
        cka_ref, cva_ref, ckb_ref, cvb_ref = (next(it) for _ in range(4))
    o_ref = next(it)
    kb_s, vb_s, ka_s, va_s = (next(it) for _ in range(4))
    if n_cache:
        cka_s, cva_s, ckb_s, cvb_s = (next(it) for _ in range(4))
    i = pl.program_id(1)

    def rope_fn(x, r0, rows):
        if not rope:
            return x
        c = cos_ref[pl.ds(r0, rows), :]
        sa = sa_ref[pl.ds(r0, rows), :]
        sb = sb_ref[pl.ds(r0, rows), :]
        return x * c + pltpu.roll(x, 96, 1) * sa + pltpu.roll(x, 32, 1) * sb

    @pl.when(i == 0)
    def _():
        for g in range(A_KV_HEADS):
            ka_s[:, g * HD:(g + 1) * HD] = rope_fn(kva_ref[:, g * HD:(g + 1) * HD], 0, L).astype(BF16)
        va_s[...] = kva_ref[:, A_KV:2 * A_KV].astype(BF16)
        for c in range(2 * B_HEADS):
            kb_s[:, c * HD:(c + 1) * HD] = rope_fn(kb_ref[:, c * HD:(c + 1) * HD], 0, L).astype(BF16)
        vb_s[...] = vb_ref[...].astype(BF16)
        if n_cache:
            cka_s[...] = cka_ref[...].astype(BF16)
            cva_s[...] = cva_ref[...].astype(BF16)
            ckb_s[...] = ckb_ref[...].astype(BF16)
            cvb_s[...] = cvb_ref[...].astype(BF16)

    q0 = pl.multiple_of(i * tq, tq)
    scale = HD ** -0.5

    if window:
        wlen = 3 * WINDOW
        start = pl.multiple_of(jnp.clip((i - 1) * WINDOW, 0, L - wlen), WINDOW)
    else:
        wlen = L
        start = 0
    for g in range(A_KV_HEADS):
        qs = [rope_fn(qa_ref[:, (g * A_GROUP + j) * HD:(g * A_GROUP + j + 1) * HD], q0, tq)
              for j in range(A_GROUP)]
        q = jnp.concatenate(qs, axis=0).astype(BF16)
        ks = ka_s[pl.ds(start, wlen), g * HD:(g + 1) * HD]
        vs = va_s[pl.ds(start, wlen), g * HD:(g + 1) * HD]
        ls = _dot_nt(q, ks) * scale
        if window:
            rows = lax.broadcasted_iota(jnp.int32, ls.shape, 0)
            cols = lax.broadcasted_iota(jnp.int32, ls.shape, 1)
            qpos = q0 + rows % tq
            kpos = start + cols
            ls = jnp.where(jnp.abs(kpos - qpos) <= WINDOW, ls, NEG)
        sink_col = jnp.concatenate(
            [jnp.full((tq, 1), sink_ref[g, j], F32) for j in range(A_GROUP)], axis=0)
        m = jnp.maximum(jnp.max(ls, axis=-1, keepdims=True), sink_col)
        if n_cache:
            lc = _dot_nt(q, cka_s[:, g * HD:(g + 1) * HD]) * scale
            m = jnp.maximum(m, jnp.max(lc, axis=-1, keepdims=True))
        ps = jnp.exp(ls - m)
        den = jnp.sum(ps, axis=-1, keepdims=True) + jnp.exp(sink_col - m)
        o = _dot(ps.astype(BF16), vs)
        if n_cache:
            pc = jnp.exp(lc - m)
            den = den + jnp.sum(pc, axis=-1, keepdims=True)
            o = o + _dot(pc.astype(BF16), cva_s[:, g * HD:(g + 1) * HD])
        o = o / den
        for j in range(A_GROUP):
            hcol = (g * A_GROUP + j) * HD
            o_ref[:, hcol:hcol + HD] = o[j * tq:(j + 1) * tq].astype(BF16)

    lam = (jnp.exp(jnp.sum(lam_ref[0:1, :] * lam_ref[1:2, :], axis=-1, keepdims=True))
           - jnp.exp(jnp.sum(lam_ref[2:3, :] * lam_ref[3:4, :], axis=-1, keepdims=True)) + lam_init)
    for h in range(B_HEADS):
        outs = []
        for w in range(2):
            col = (2 * h + w) * HD
            q = rope_fn(qb_ref[:, col:col + HD], q0, tq).astype(BF16)
            ls = _dot_nt(q, kb_s[:, col:col + HD]) * scale
            m = jnp.max(ls, axis=-1, keepdims=True)
            if n_cache:
                lc = _dot_nt(q, ckb_s[:, col:col + HD]) * scale
                m = jnp.maximum(m, jnp.max(lc, axis=-1, keepdims=True))
            ps = jnp.exp(ls - m)
            den = jnp.sum(ps, axis=-1, keepdims=True)
            o = _dot(ps.astype(BF16), vb_s[:, h * B_VD:(h + 1) * B_VD])
            if n_cache:
                pc = jnp.exp(lc - m)
                den = den + jnp.sum(pc, axis=-1, keepdims=True)
                o = o + _dot(pc.astype(BF16), cvb_s[:, h * B_VD:(h + 1) * B_VD])
            outs.append(o / den)
        ob = outs[0] - lam * outs[1]
        y = ob * lax.rsqrt(jnp.mean(ob * ob, axis=-1, keepdims=True) + EPS)
        y = y * sub_ref[...] * (1.0 - lam_init)
        o_ref[:, A_Q + h * B_VD:A_Q + (h + 1) * B_VD] = y.astype(BF16)


def _attention(p, row0, nb, L, tq, sink, lam_vecs, subln_g, lam_init, caches=None, rope_tabs=None):
    assert row0 % L == 0 and L % tq == 0
    nq = L // tq
    qb0 = row0 // tq
    kb0 = row0 // L
    n_cache = 0 if caches is None else caches[0].shape[1]
    rope = rope_tabs is not None
    in_specs = [
        pl.BlockSpec(memory_space=pltpu.SMEM),
        pl.BlockSpec((4, HD), lambda b, i: (0, 0)),
        pl.BlockSpec((1, B_VD), lambda b, i: (0, 0)),
    ]
    args = [sink, lam_vecs, subln_g.reshape(1, B_VD)]
    if rope:
        in_specs += [pl.BlockSpec((L, HD), lambda b, i: (0, 0))] * 3
        args += list(rope_tabs)
    in_specs += [
        pl.BlockSpec((tq, A_Q), lambda b, i: (qb0 + b * nq + i, 0)),
        pl.BlockSpec((tq, B_QK), lambda b, i: (qb0 + b * nq + i, 1)),
        pl.BlockSpec((L, B_QK), lambda b, i: (kb0 + b, 2)),
        pl.BlockSpec((L, B_V), lambda b, i: (kb0 + b, 3)),
        pl.BlockSpec((L, 2 * A_KV), lambda b, i: (kb0 + b, (A_Q + 2 * B_QK + B_V) // (2 * A_KV))),
    ]
    args += [p] * 5
    scratch = [pltpu.VMEM((L, B_QK), BF16), pltpu.VMEM((L, B_V), BF16),
               pltpu.VMEM((L, A_KV), BF16), pltpu.VMEM((L, A_KV), BF16)]
    if n_cache:
        for cch in caches:
            in_specs.append(pl.BlockSpec((None,) + cch.shape[1:], lambda b, i: (b, 0, 0)))
            scratch.append(pltpu.VMEM(cch.shape[1:], BF16))
        args += list(caches)
    return pl.pallas_call(
        functools.partial(_attn_kernel, L=L, tq=tq, n_cache=n_cache, rope=rope,
                          window=rope, lam_init=lam_init),
        grid=(nb, nq),
        in_specs=in_specs,
        out_specs=pl.BlockSpec((tq, ATTN_MIX), lambda b, i: (b * nq + i, 0)),
        out_shape=jax.ShapeDtypeStruct((nb * L, ATTN_MIX), BF16),
        scratch_shapes=scratch,
        compiler_params=_cparams("parallel", "arbitrary"),
    )(*args)


def _rope_tables(L):
    rows = L // GRID_W
    row = jnp.repeat(jnp.arange(rows), GRID_W).astype(F32)
    col = jnp.tile(jnp.arange(GRID_W), rows).astype(F32)
    half = HD // 2
    inv = ROPE_BASE ** (-jnp.arange(0, half, 2, dtype=F32) / half)
    ar = row[:, None] * inv
    ac = col[:, None] * inv
    ang = jnp.concatenate([ar, ar, ac, ac], axis=-1)
    cos, sin = jnp.cos(ang), jnp.sin(ang)
    first = (jnp.arange(HD) % (HD // 2)) < (HD // 4)
    return cos, jnp.where(first, -sin, 0.0), jnp.where(first, 0.0, sin)


def _conv_kernel(x_ref, w_ref, b_ref, o_ref, *, L):
    x = x_ref[...]
    t = lax.broadcasted_iota(jnp.int32, x.shape, 0)
    acc = b_ref[...] + x * w_ref[CONV_K // 2:CONV_K // 2 + 1, :]
    for k in range(CONV_K):
        d = k - CONV_K // 2
        if d == 0:
            continue
        xs = pltpu.roll(x, (-d) % L, 0)
        valid = jnp.logical_and(t + d >= 0, t + d < L)
        acc = acc + jnp.where(valid, xs, 0.0) * w_ref[k:k + 1, :]
    o_ref[...] = acc * _sigmoid(acc)


def _conv_silu(zx, row0, nb, L, conv_w, conv_b):
    tc = 512
    kb0 = row0 // L
    c0 = D_INNER // tc
    return pl.pallas_call(
        functools.partial(_conv_kernel, L=L),
        grid=(nb, CONV_CH // tc),
        in_specs=[
            pl.BlockSpec((L, tc), lambda b, j: (kb0 + b, c0 + j)),
            pl.BlockSpec((CONV_K, tc), lambda b, j: (0, j)),
            pl.BlockSpec((1, tc), lambda b, j: (0, j)),
        ],
        out_specs=pl.BlockSpec((L, tc), lambda b, j: (b, j)),
        out_shape=jax.ShapeDtypeStruct((nb * L, CONV_CH), F32),
        compiler_params=_cparams("parallel", "parallel"),
    )(zx, conv_w, conv_b.reshape(1, CONV_CH))


def _ssd_kernel(*refs, has_h0):
    it = iter(refs)
    x_ref, b_ref, c_ref, dt_ref, dtb_ref, alog_ref = (next(it) for _ in range(6))
    if has_h0:
        h0_ref = next(it)
    y_ref, h_ref = next(it), next(it)
    fwd = pl.program_id(1) == 0
    c = pl.program_id(2)

    @pl.when(c == 0)
    def _():
        if has_h0:
            h_ref[...] = h0_ref[...]
        else:
            h_ref[...] = jnp.zeros_like(h_ref)

    r = lax.broadcasted_iota(jnp.int32, (CHUNK, CHUNK), 0)
    s = lax.broadcasted_iota(jnp.int32, (CHUNK, CHUNK), 1)
    incl = jnp.where(fwd, r - s, s - r) >= 0
    raw = dt_ref[...] + dtb_ref[...]
    dt = jnp.maximum(raw, 0.0) + jnp.log(1.0 + jnp.exp(-jnp.abs(raw)))
    dta = dt * (-jnp.exp(alog_ref[...]))
    ac = _dot(jnp.where(incl, 1.0, 0.0), dta, precision=HIGHEST)
    ac_last = jnp.where(fwd, ac[CHUNK - 1:CHUNK, :], ac[0:1, :])
    ac_t = jnp.transpose(jnp.concatenate([ac, jnp.zeros_like(ac)], axis=1))
    eac = jnp.exp(ac)
    dout = jnp.exp(ac_last - ac)
    cdec = jnp.exp(ac_last)

    eh = lax.broadcasted_iota(jnp.int32, (SSD_HEADS, D_INNER), 0)
    ec = lax.broadcasted_iota(jnp.int32, (SSD_HEADS, D_INNER), 1)
    expand = jnp.where(ec // SSD_HEADDIM == eh, 1.0, 0.0)
    dt_e = _dot(dt, expand, precision=HIGHEST)
    eac_e = _dot(eac, expand, precision=HIGHEST)
    dout_e = _dot(dout, expand, precision=HIGHEST)

    dtx = x_ref[...] * dt_e
    dtxd_b = (dtx * dout_e).astype(BF16)
    gw = SSD_HPG * SSD_HEADDIM
    for g in range(SSD_GROUPS):
        bg = b_ref[:, g * SSD_STATE:(g + 1) * SSD_STATE].astype(BF16)
        cg = c_ref[:, g * SSD_STATE:(g + 1) * SSD_STATE].astype(BF16)
        scores = _dot_nt(cg, bg)
        h_in = h_ref[g * gw:(g + 1) * gw, :]
        y_off = _dot_nt(cg, h_in.astype(BF16)) * eac_e[:, g * gw:(g + 1) * gw]
        yd = []
        for hh in range(SSD_HPG):
            h = g * SSD_HPG + hh
            decay = jnp.exp(jnp.where(incl, ac[:, h:h + 1] - ac_t[h:h + 1, :], NEG))
            mm = (scores * decay).astype(BF16)
            yd.append(_dot(mm, dtx[:, h * SSD_HEADDIM:(h + 1) * SSD_HEADDIM].astype(BF16)))
        y_ref[:, g * gw:(g + 1) * gw] = jnp.concatenate(yd, axis=1) + y_off
        st = _dot_tn(dtxd_b[:, g * gw:(g + 1) * gw], bg)
        for hh in range(SSD_HPG):
            h = g * SSD_HPG + hh
            rows = slice(h * SSD_HEADDIM, (h + 1) * SSD_HEADDIM)
            h_ref[rows, :] = cdec[:, h:h + 1] * h_in[hh * SSD_HEADDIM:(hh + 1) * SSD_HEADDIM, :] \
                + st[hh * SSD_HEADDIM:(hh + 1) * SSD_HEADDIM, :]


def _ssd_scan(xc, dt2, dt_bias, a_log, nb, L, h0=None):
    nc = L // CHUNK
    has_h0 = h0 is not None
    sw = SSD_GROUPS * SSD_STATE

    def rowblk(b, d, c):
        return b * nc + jnp.where(d == 0, c, nc - 1 - c)

    in_specs = [
        pl.BlockSpec((CHUNK, D_INNER), lambda b, d, c: (rowblk(b, d, c), 0)),
        pl.BlockSpec((CHUNK, sw), lambda b, d, c: (rowblk(b, d, c), D_INNER // sw)),
        pl.BlockSpec((CHUNK, sw), lambda b, d, c: (rowblk(b, d, c), D_INNER // sw + 1)),
        pl.BlockSpec((None, CHUNK, SSD_HEADS), lambda b, d, c: (d, rowblk(b, d, c), 0)),
        pl.BlockSpec((None, 1, SSD_HEADS), lambda b, d, c: (d, 0, 0)),
        pl.BlockSpec((None, 1, SSD_HEADS), lambda b, d, c: (d, 0, 0)),
    ]
    args = [xc, xc, xc, dt2, dt_bias.reshape(2, 1, SSD_HEADS), a_log.reshape(2, 1, SSD_HEADS)]
    if has_h0:
        in_specs.append(pl.BlockSpec((None, None, D_INNER, SSD_STATE), lambda b, d, c: (d, b, 0, 0)))
        args.append(h0)
    return pl.pallas_call(
        functools.partial(_ssd_kernel, has_h0=has_h0),
        grid=(nb, 2, nc),
        in_specs=in_specs,
        out_specs=[
            pl.BlockSpec((None, CHUNK, D_INNER), lambda b, d, c: (d, rowblk(b, d, c), 0)),
            pl.BlockSpec((None, None, D_INNER, SSD_STATE), lambda b, d, c: (d, b, 0, 0)),
        ],
        out_shape=[
            jax.ShapeDtypeStruct((2, nb * L, D_INNER), F32),
            jax.ShapeDtypeStruct((2, nb, D_INNER, SSD_STATE), F32),
        ],
        compiler_params=_cparams("parallel", "parallel", "arbitrary"),
    )(*args)


def _ssd_gate_kernel(yf_ref, yb_ref, xs_ref, z_ref, d_ref, g_ref, o_ref):
    z = z_ref[...]
    y = (yf_ref[...] + yb_ref[...] + d_ref[...] * xs_ref[...]) * (z * _sigmoid(z))
    gs = D_INNER // SSD_GROUPS
    for g in range(SSD_GROUPS):
        yg = y[:, g * gs:(g + 1) * gs]
        n = yg * lax.rsqrt(jnp.mean(yg * yg, axis=-1, keepdims=True) + EPS)
        o_ref[:, g * gs:(g + 1) * gs] = (n * g_ref[:, g * gs:(g + 1) * gs]).astype(BF16)


def _ssd_gate(y2, xc, zx, d_skip, norm_g):
    n = xc.shape[0]
    tm = 128
    d_e = jnp.repeat(d_skip, SSD_HEADDIM).reshape(1, D_INNER)
    return pl.pallas_call(
        _ssd_gate_kernel,
        grid=(n // tm,),
        in_specs=[
            pl.BlockSpec((None, tm, D_INNER), lambda i: (0, i, 0)),
            pl.BlockSpec((None, tm, D_INNER), lambda i: (1, i, 0)),
            pl.BlockSpec((tm, D_INNER), lambda i: (i, 0)),
            pl.BlockSpec((tm, D_INNER), lambda i: (i, 0)),
            pl.BlockSpec((1, D_INNER), lambda i: (0, 0)),
            pl.BlockSpec((1, D_INNER), lambda i: (0, 0)),
        ],
        out_specs=pl.BlockSpec((tm, D_INNER), lambda i: (i, 0)),
        out_shape=jax.ShapeDtypeStruct((n, D_INNER), BF16),
        compiler_params=_cparams("parallel"),
    )(y2, y2, xc, zx, d_e, norm_g.reshape(1, D_INNER))


def _topk_kernel(q_ref, sk_ref, s1_ref, s2_ref, aux_ref, s_scr, sv_scr):
    tk = q_ref.shape[0]
    for hc in range(2 * PEER_HEADS):
        qh = q_ref[:, hc * PEER_DHALF:(hc + 1) * PEER_DHALF].astype(BF16)
        s_t = _dot_nt(sk_ref[hc].astype(BF16), qh)
        s_scr[hc] = s_t
        if hc % 2 == 0:
            s1_ref[hc // 2] = s_t
        else:
            s2_ref[hc // 2] = s_t

    slot = lax.broadcasted_iota(jnp.int32, (PEER_TOPK, tk), 0).astype(F32)
    ninf = -jnp.inf

    def extract(x):
        m = jnp.max(x, axis=0, keepdims=True)
        eq = x == m
        cnt = jnp.sum(jnp.where(eq, 1.0, 0.0), axis=0, keepdims=True)
        return m, cnt, jnp.where(eq, ninf, x)

    def stage1(hc, carry):
        def step(_, st):
            x, sv, n = st
            m, cnt, x = extract(x)
            sv = jnp.where(jnp.logical_and(slot >= n, slot < n + cnt), m, sv)
            return x, sv, n + cnt

        _, sv, _ = lax.fori_loop(
            0, PEER_TOPK, step,
            (s_scr[hc], jnp.full((PEER_TOPK, tk), ninf, F32), jnp.zeros((1, tk), F32)))
        sv_scr[hc] = sv
        return carry

    lax.fori_loop(0, 2 * PEER_HEADS, stage1, 0)

    def stage2(h, carry):
        sv1 = sv_scr[2 * h]
        sv2 = sv_scr[2 * h + 1]
        cand = jnp.concatenate([sv1[i:i + 1, :] + sv2 for i in range(PEER_TOPK)], axis=0)
        m0 = sv1[0:1, :] + sv2[0:1, :]

        def step(_, st):
            x, thr, z, n = st
            m, cnt, x = extract(x)
            take = jnp.clip(PEER_TOPK - n, 0.0, cnt)
            z = z + take * jnp.exp(m - m0)
            thr = jnp.where(n < PEER_TOPK, m, thr)
            return x, thr, z, n + cnt

        _, thr, z, _ = lax.fori_loop(
            0, PEER_TOPK, step, (cand, m0, jnp.zeros((1, tk), F32), jnp.zeros((1, tk), F32)))
        aux_ref[0, pl.ds(h, 1), :] = thr
        aux_ref[1, pl.ds(h, 1), :] = sv1[0:1, :]
        aux_ref[2, pl.ds(h, 1), :] = sv2[0:1, :]
        aux_ref[3, pl.ds(h, 1), :] = z
        return carry

    lax.fori_loop(0, PEER_HEADS, stage2, 0)


def _peer_topk(q, sub_keys):
    n = q.shape[0]
    tk = LANES
    sk = sub_keys.reshape(2 * PEER_HEADS, PEER_KEYS, PEER_DHALF)
    return pl.pallas_call(
        _topk_kernel,
        grid=(n // tk,),
        in_specs=[
            pl.BlockSpec((tk, PEER_HEADS * PEER_DKEY), lambda i: (i, 0)),
            pl.BlockSpec(sk.shape, lambda i: (0, 0, 0)),
        ],
        out_specs=[
            pl.BlockSpec((PEER_HEADS, PEER_KEYS, tk), lambda i: (0, 0, i)),
            pl.BlockSpec((PEER_HEADS, PEER_KEYS, tk), lambda i: (0, 0, i)),
            pl.BlockSpec((4, PEER_HEADS, tk), lambda i: (0, 0, i)),
        ],
        out_shape=[
            jax.ShapeDtypeStruct((PEER_HEADS, PEER_KEYS, n), F32),
            jax.ShapeDtypeStruct((PEER_HEADS, PEER_KEYS, n), F32),
            jax.ShapeDtypeStruct((4, PEER_HEADS, n), F32),
        ],
        scratch_shapes=[
            pltpu.VMEM((2 * PEER_HEADS, PEER_KEYS, tk), F32),
            pltpu.VMEM((2 * PEER_HEADS, PEER_TOPK, tk), F32),
        ],
        compiler_params=_cparams("parallel"),
    )(q, sk)


def _gelu_tanh(x):
    return 0.5 * x * (1.0 + jnp.tanh(math.sqrt(2.0 / math.pi) * (x + 0.044715 * (x * x * x))))


def _peer_kernel(hb_ref, u_ref, vt_ref, s1_ref, s2_ref, aux_ref, x_ref, gate_ref, o_ref,
                 acc, st, act, a2):
    e = pl.program_id(1)
    n_sub = u_ref.shape[0] // PEER_KEYS

    @pl.when(e == 0)
    def _():
        acc[...] = jnp.zeros_like(acc)
        for h in range(PEER_HEADS):
            a2[h] = jnp.exp(s2_ref[h] - aux_ref[2, h:h + 1, :])

    st[...] = _dot_nt(u_ref[...], hb_ref[...])
    thr = aux_ref[0]
    m1 = aux_ref[1]
    rz = 1.0 / aux_ref[3]

    def body(a, carry):
        off = pl.multiple_of(a * PEER_KEYS, PEER_KEYS)
        w = jnp.zeros((PEER_KEYS, st.shape[1]), F32)
        for h in range(PEER_HEADS):
            s1row = s1_ref[h, pl.ds(a, 1), :]
            a1row = jnp.exp(s1row - m1[h:h + 1, :]) * rz[h:h + 1, :]
            sel = (s1row + s2_ref[h]) >= thr[h:h + 1, :]
            w = w + jnp.where(sel, a2[h], 0.0) * a1row
        act[pl.ds(off, PEER_KEYS), :] = (_gelu_tanh(st[pl.ds(off, PEER_KEYS), :]) * w).astype(BF16)
        return carry

    lax.fori_loop(0, n_sub, body, 0)
    acc[...] += _dot(vt_ref[...], act[...])

    @pl.when(e == pl.num_programs(1) - 1)
    def _():
        o_ref[...] = x_ref[...] + gate_ref[0] * jnp.transpose(acc[...])


def _peer_dense(hb, u_b, vt_b, s1, s2, aux, x, mod, gate_k, n_ctx, ld, tm=512, te=1024):
    n, d = x.shape
    cond = functools.partial(_cond_of_tile, tm=tm, n_ctx=n_ctx, ld=ld)
    return pl.pallas_call(
        _peer_kernel,
        grid=(n // tm, PEER_EXPERTS // te),
        in_specs=[
            pl.BlockSpec((tm, d), lambda i, e: (i, 0)),
            pl.BlockSpec((te, d), lambda i, e: (e, 0)),
            pl.BlockSpec((d, te), lambda i, e: (0, e)),
            pl.BlockSpec((PEER_HEADS, te // PEER_KEYS, tm), lambda i, e: (0, e, i)),
            pl.BlockSpec((PEER_HEADS, PEER_KEYS, tm), lambda i, e: (0, 0, i)),
            pl.BlockSpec((4, PEER_HEADS, tm), lambda i, e: (0, 0, i)),
            pl.BlockSpec((tm, d), lambda i, e: (i, 0)),
            pl.BlockSpec((1, 1, d), lambda i, e: (cond(i) * 6 + gate_k, 0, 0)),
        ],
        out_specs=pl.BlockSpec((tm, d), lambda i, e: (i, 0)),
        out_shape=jax.ShapeDtypeStruct((n, d), F32),
        scratch_shapes=[
            pltpu.VMEM((d, tm), F32),
            pltpu.VMEM((te, tm), F32),
            pltpu.VMEM((te, tm), BF16),
            pltpu.VMEM((PEER_HEADS, PEER_KEYS, tm), F32),
        ],
        compiler_params=_cparams("parallel", "arbitrary"),
    )(hb, u_b, vt_b, s1, s2, aux, x, mod)


def _peer_layer(x, mod, norm_g, w_q, sub_keys, u, v, n_ctx, ld, tm):
    q, hb = _proj(x, w_q.astype(BF16), tm=tm, tn=512, n_ctx=n_ctx, ld=ld, mod=mod, norm_g=norm_g,
                  shift_k=3, scale_k=4, emit_h=True)
    s1, s2, aux = _peer_topk(q, sub_keys)
    return _peer_dense(hb, u.astype(BF16), jnp.transpose(v).astype(BF16), s1, s2, aux, x, mod, 5,
                       n_ctx, ld)


def _rmsnorm_kernel(x_ref, g_ref, o_ref):
    x = x_ref[...]
    o_ref[...] = x * lax.rsqrt(jnp.mean(x * x, axis=-1, keepdims=True) + EPS) * g_ref[...]


def _final_norm(x, g):
    n, d = x.shape
    tm = 512
    return pl.pallas_call(
        _rmsnorm_kernel,
        grid=(n // tm,),
        in_specs=[pl.BlockSpec((tm, d), lambda i: (i, 0)), pl.BlockSpec((1, d), lambda i: (0, 0))],
        out_specs=pl.BlockSpec((tm, d), lambda i: (i, 0)),
        out_shape=jax.ShapeDtypeStruct((n, d), F32),
        compiler_params=_cparams("parallel"),
    )(x, g.reshape(1, d))


def _diff_lambda_init(layer):
    return 0.8 - 0.6 * math.exp(-0.3 * layer)


def kernel(x_prompt, x_sample, cache_a_k, cache_a_v, cache_b_k, cache_b_v, state_ssd_fwd, state_ssd_bwd, c, c_ctx, w_mod, b_mod, norm_g, attn_w_in, attn_w_out, attn_sink, diff_lam, diff_subln_g, ssd_w_in, ssd_conv_w, ssd_conv_b, ssd_dt_bias, ssd_a_log, ssd_d, ssd_norm_g, ssd_w_out, peer_w_q, peer_sub_keys, peer_u, peer_v, final_g):
    bp, lp, d = x_prompt.shape
    bd, ld, _ = x_sample.shape
    past = cache_a_k.shape[2]
    n_ctx, n_lat = bp * lp, bd * ld
    tm = 512
    assert n_ctx % ld == 0 and n_ctx % tm == 0 and ld % tm == 0

    x = jnp.concatenate([x_prompt.reshape(n_ctx, d), x_sample.reshape(n_lat, d)], axis=0)

    r = -(-(bd + 1) // SUBLANES) * SUBLANES
    cond = jnp.concatenate([c_ctx[None], c, jnp.zeros((r - bd - 1, d), F32)], axis=0)
    mod_all = _modulation(cond, w_mod, b_mod).reshape(DEPTH, r * 6, 1, d)

    outs = {}
    for l in range(DEPTH):
        i = l // 2
        mod = mod_all[l]
        if l % 2 == 0:
            lam_init = _diff_lambda_init(l)
            w_in = attn_w_in[i]
            sp = (0, A_Q, A_Q + A_KV, A_Q + 2 * A_KV, A_Q + 2 * A_KV + B_QK, A_Q + 2 * A_KV + 2 * B_QK, ATTN_IN)
            aq, ak, av, bq, bk, bv = (w_in[:, sp[k]:sp[k + 1]] for k in range(6))
            w_perm = jnp.concatenate([aq, bq, bk, bv, ak, av], axis=1).astype(BF16)
            p = _proj(x, w_perm, tm=tm, tn=512, n_ctx=n_ctx, ld=ld, mod=mod, norm_g=norm_g[l, 0],
                      shift_k=0, scale_k=1)
            mix_c = _attention(p, 0, bp, lp, lp, attn_sink[i], diff_lam[i], diff_subln_g[i], lam_init)
            caches = (cache_a_k[:, i].reshape(bd, past, A_KV), cache_a_v[:, i].reshape(bd, past, A_KV),
                      cache_b_k[:, i].reshape(bd, past, B_QK), cache_b_v[:, i].reshape(bd, past, B_V))
            mix_l = _attention(p, n_ctx, bd, ld, WINDOW, attn_sink[i], diff_lam[i], diff_subln_g[i],
                               lam_init, caches=caches, rope_tabs=_rope_tables(ld))
            mixed = jnp.concatenate([mix_c, mix_l], axis=0)
            x = _proj(mixed, attn_w_out[i].astype(BF16), tm=tm, tn=512, n_ctx=n_ctx, ld=ld, mod=mod,
                      res=x, gate_k=2)
            c0 = A_Q + 2 * B_QK + B_V
            pc = p[:n_ctx]
            outs.setdefault("ak", []).append(pc[:, c0:c0 + A_KV].reshape(bp, lp, A_KV_HEADS, HD))
            outs.setdefault("av", []).append(pc[:, c0 + A_KV:c0 + 2 * A_KV].reshape(bp, lp, A_KV_HEADS, HD))
            outs.setdefault("bk", []).append(pc[:, A_Q + B_QK:A_Q + 2 * B_QK].reshape(bp, lp, B_HEADS, 2, HD))
            outs.setdefault("bv", []).append(pc[:, A_Q + 2 * B_QK:c0].reshape(bp, lp, B_HEADS, B_VD))
        else:
            w_in = ssd_w_in[i]
            nzx = D_INNER + CONV_CH
            zx = _proj(x, w_in[:, :nzx].astype(BF16), tm=tm, tn=1024, n_ctx=n_ctx, ld=ld, mod=mod,
                       norm_g=norm_g[l, 0], shift_k=0, scale_k=1)
            dtr = _proj(x, w_in[:, nzx:].astype(BF16), tm=tm, tn=2 * SSD_HEADS, n_ctx=n_ctx, ld=ld,
                        mod=mod, norm_g=norm_g[l, 0], shift_k=0, scale_k=1)
            dt2 = jnp.transpose(dtr.reshape(-1, 2, SSD_HEADS), (1, 0, 2))
            xc_c = _conv_silu(zx, 0, bp, lp, ssd_conv_w[i], ssd_conv_b[i])
            xc_l = _conv_silu(zx, n_ctx, bd, ld, ssd_conv_w[i], ssd_conv_b[i])
            y_c, h_c = _ssd_scan(xc_c, dt2[:, :n_ctx], ssd_dt_bias[i], ssd_a_log[i], bp, lp)
            h0 = jnp.stack([state_ssd_fwd[:, i], state_ssd_bwd[:, i]]).reshape(2, bd, D_INNER, SSD_STATE)
            y_l, _ = _ssd_scan(xc_l, dt2[:, n_ctx:], ssd_dt_bias[i], ssd_a_log[i], bd, ld, h0=h0)
            y2 = jnp.concatenate([y_c, y_l], axis=1)
            xc = jnp.concatenate([xc_c, xc_l], axis=0)
            hg = _ssd_gate(y2, xc, zx, ssd_d[i], ssd_norm_g[i])
            x = _proj(hg, ssd_w_out[i].astype(BF16), tm=tm, tn=512, n_ctx=n_ctx, ld=ld, mod=mod,
                      res=x, gate_k=2)
            hs = h_c.reshape(2, bp, SSD_HEADS, SSD_HEADDIM, SSD_STATE)
            outs.setdefault("sf", []).append(hs[0])
            outs.setdefault("sb", []).append(hs[1])
        x = _peer_layer(x, mod, norm_g[l, 1], peer_w_q[l], peer_sub_keys[l], peer_u[l], peer_v[l],
                        n_ctx, ld, tm)

    y = _final_norm(x, final_g)
    y_prompt = y[:n_ctx].reshape(bp, lp, d)
    y_sample = y[n_ctx:].reshape(bd, ld, d)
    return (y_prompt, y_sample, jnp.stack(outs["ak"], axis=1), jnp.stack(outs["av"], axis=1),
            jnp.stack(outs["bk"], axis=1), jnp.stack(outs["bv"], axis=1),
            jnp.stack(outs["sf"], axis=1), jnp.stack(outs["sb"], axis=1))
```

```python
import functools
import math

import jax
import jax.numpy as jnp
from jax import lax
from jax.experimental import pallas as pl
from jax.experimental.pallas import tpu as pltpu

F32 = jnp.float32
BF16 = jnp.bfloat16

D_MODEL = 2048
DEPTH = 2
GRID_W = 64
EPS = 1e-6
NEG = -1e30

HD = 128
A_HEADS = 8
A_KV_HEADS = 2
A_GROUP = A_HEADS // A_KV_HEADS
WINDOW = 128
B_HEADS = 4
B_VD = 2 * HD
A_Q = A_HEADS * HD
A_KV = A_KV_HEADS * HD
B_QK = B_HEADS * 2 * HD
B_V = B_HEADS * B_VD
ATTN_IN = A_Q + 2 * A_KV + 2 * B_QK + B_V
ATTN_MIX = A_Q + B_V
ROPE_BASE = 10000.0

D_INNER = 2 * D_MODEL
SSD_HEADDIM = 64
SSD_HEADS = D_INNER // SSD_HEADDIM
SSD_GROUPS = 8
SSD_HPG = SSD_HEADS // SSD_GROUPS
SSD_STATE = 128
CONV_K = 5
CONV_CH = D_INNER + 2 * SSD_GROUPS * SSD_STATE
CHUNK = 128

PEER_HEADS = 8
PEER_KEYS = 128
PEER_EXPERTS = PEER_KEYS * PEER_KEYS
PEER_TOPK = 16
PEER_DKEY = 256
PEER_DHALF = PEER_DKEY // 2

V7X_VMEM_LIMIT_BYTES = 56 * 1024 * 1024
LANES = 128
SUBLANES = 8

HIGHEST = lax.Precision.HIGHEST


def _cparams(*sem):
    return pltpu.CompilerParams(dimension_semantics=sem, vmem_limit_bytes=V7X_VMEM_LIMIT_BYTES)


def _dot(a, b, precision=None):
    return jnp.dot(a, b, preferred_element_type=F32, precision=precision)


def _dot_nt(a, b):
    return lax.dot_general(a, b, (((1,), (1,)), ((), ())), preferred_element_type=F32)


def _dot_tn(a, b):
    return lax.dot_general(a, b, (((0,), (0,)), ((), ())), preferred_element_type=F32)


def _sigmoid(x):
    return 1.0 / (1.0 + jnp.exp(-x))


def _cond_of_tile(i, tm, n_ctx, ld):
    r = i * tm
    return jnp.where(r < n_ctx, 0, 1 + (r - n_ctx) // ld)


def _mod_kernel(c_ref, w_ref, b_ref, o_ref):
    c = c_ref[...]
    s = c * _sigmoid(c)
    o_ref[...] = _dot(s.astype(BF16), w_ref[...].astype(BF16)) + b_ref[...]


def _modulation(cond, w_mod, b_mod):
    depth, d, n6 = w_mod.shape
    r = cond.shape[0]
    tn = 1024
    return pl.pallas_call(
        _mod_kernel,
        grid=(depth, n6 // tn),
        in_specs=[
            pl.BlockSpec((r, d), lambda l, j: (0, 0)),
            pl.BlockSpec((None, d, tn), lambda l, j: (l, 0, j)),
            pl.BlockSpec((None, 1, tn), lambda l, j: (l, 0, j)),
        ],
        out_specs=pl.BlockSpec((None, r, tn), lambda l, j: (l, 0, j)),
        out_shape=jax.ShapeDtypeStruct((depth, r, n6), F32),
        compiler_params=_cparams("parallel", "parallel"),
        name="modulation",
    )(cond, w_mod, b_mod.reshape(depth, 1, n6))


def _proj_kernel(*refs, adaln, emit_h, residual, split_tiles):
    it = iter(refs)
    x_ref = next(it)
    if split_tiles:
        x2_ref = next(it)
    if adaln:
        g_ref, sh_ref, sc_ref = next(it), next(it), next(it)
    w_ref = next(it)
    if residual:
        r_ref, gate_ref = next(it), next(it)
    o_ref = next(it)
    if emit_h:
        h_ref = next(it)
    if adaln:
        hs_ref = next(it)
        j = pl.program_id(1)

        @pl.when(j == 0)
        def _():
            x = x_ref[...]
            y = x * lax.rsqrt(jnp.mean(x * x, axis=-1, keepdims=True) + EPS)
            y = y * g_ref[...]
            y = y * (1.0 + sc_ref[0]) + sh_ref[0]
            hb = y.astype(BF16)
            hs_ref[...] = hb
            if emit_h:
                h_ref[...] = hb

        h = hs_ref[...]
    elif split_tiles:
        h = jnp.where(pl.program_id(0) < split_tiles, x_ref[...], x2_ref[...])
    else:
        h = x_ref[...]
    acc = _dot(h, w_ref[...])
    if residual:
        o_ref[...] = r_ref[...] + gate_ref[0] * acc
    else:
        o_ref[...] = acc


def _proj(x, w, *, tm, tn, n_ctx, ld, name, mod=None, norm_g=None, shift_k=None, scale_k=None,
          emit_h=False, res=None, gate_k=None):
    split = isinstance(x, tuple)
    if split:
        x, x2 = x
        split_tiles = x.shape[0] // tm
        n, k = x.shape[0] + x2.shape[0], x.shape[1]
    else:
        split_tiles = 0
        n, k = x.shape
    m = w.shape[1]
    adaln = norm_g is not None
    residual = res is not None
    cond = functools.partial(_cond_of_tile, tm=tm, n_ctx=n_ctx, ld=ld)
    if split:
        in_specs = [pl.BlockSpec((tm, k), lambda i, j: (jnp.minimum(i, split_tiles - 1), 0)),
                    pl.BlockSpec((tm, k), lambda i, j: (jnp.maximum(i - split_tiles, 0), 0))]
        args = [x, x2]
    else:
        in_specs = [pl.BlockSpec((tm, k), lambda i, j: (i, 0))]
        args = [x]
    if adaln:
        in_specs += [
            pl.BlockSpec((1, k), lambda i, j: (0, 0)),
            pl.BlockSpec((1, 1, k), lambda i, j: (cond(i) * 6 + shift_k, 0, 0)),
            pl.BlockSpec((1, 1, k), lambda i, j: (cond(i) * 6 + scale_k, 0, 0)),
        ]
        args += [norm_g.reshape(1, k), mod, mod]
    in_specs.append(pl.BlockSpec((k, tn), lambda i, j: (0, j)))
    args.append(w)
    if residual:
        in_specs += [
            pl.BlockSpec((tm, tn), lambda i, j: (i, j)),
            pl.BlockSpec((1, 1, tn), lambda i, j: (cond(i) * 6 + gate_k, 0, j)),
        ]
        args += [res, mod]
    out_specs = [pl.BlockSpec((tm, tn), lambda i, j: (i, j))]
    out_shape = [jax.ShapeDtypeStruct((n, m), F32)]
    if emit_h:
        out_specs.append(pl.BlockSpec((tm, k), lambda i, j: (i, 0)))
        out_shape.append(jax.ShapeDtypeStruct((n, k), BF16))
    scratch = [pltpu.VMEM((tm, k), BF16)] if adaln else []
    outs = pl.pallas_call(
        functools.partial(_proj_kernel, adaln=adaln, emit_h=emit_h, residual=residual,
                          split_tiles=split_tiles),
        grid=(n // tm, m // tn),
        in_specs=in_specs,
        out_specs=out_specs,
        out_shape=out_shape,
        scratch_shapes=scratch,
        compiler_params=_cparams("parallel", "arbitrary"),
        name=name,
    )(*args)
    return outs if emit_h else outs[0]


def _attn_kernel(*refs, L, tq, n_cache, rope, window, lam_init):
    it = iter(refs)
    sink_ref = next(it)
    lam_ref = next(it)
    sub_ref = next(it)
    if rope:
        cos_ref, sa_ref, sb_ref = next(it), next(it), next(it)
    qa_ref, qb_ref, kb_ref, vb_ref, kva_ref = (next(it) for _ in range(5))
    if n_cache:
        cka_ref, cva_ref, ckb_ref, cvb_ref = (next(it) for _ in range(4))
    o_ref = next(it)
    kb_s, vb_s, ka_s, va_s = (next(it) for _ in range(4))
    if n_cache:
        cka_s, cva_s, ckb_s, cvb_s = (next(it) for _ in range(4))
    i = pl.program_id(1)

    def rope_fn(x, r0, rows):
        if not rope:
            return x
        c = cos_ref[pl.ds(r0, rows), :]
        sa = sa_ref[pl.ds(r0, rows), :]
        sb = sb_ref[pl.ds(r0, rows), :]
        return x * c + pltpu.roll(x, 96, 1) * sa + pltpu.roll(x, 32, 1) * sb

    @pl.when(i == 0)
    def _():
        for g in range(A_KV_HEADS):
            ka_s[:, g * HD:(g + 1) * HD] = rope_fn(kva_ref[:, g * HD:(g + 1) * HD], 0, L).astype(BF16)
        va_s[...] = kva_ref[:, A_KV:2 * A_KV].astype(BF16)
        for c in range(2 * B_HEADS):
            kb_s[:, c * HD:(c + 1) * HD] = rope_fn(kb_ref[:, c * HD:(c + 1) * HD], 0, L).astype(BF16)
        vb_s[...] = vb_ref[...].astype(BF16)
        if n_cache:
            cka_s[...] = cka_ref[...].astype(BF16)
            cva_s[...] = cva_ref[...].astype(BF16)
            ckb_s[...] = ckb_ref[...].astype(BF16)
            cvb_s[...] = cvb_ref[...].astype(BF16)

    q0 = pl.multiple_of(i * tq, tq)
    scale = HD ** -0.5

    if window:
        wlen = 3 * WINDOW
        start = pl.multiple_of(jnp.clip((i - 1) * WINDOW, 0, L - wlen), WINDOW)
    else:
        wlen = L
        start = 0
    for g in range(A_KV_HEADS):
        qs = [rope_fn(qa_ref[:, (g * A_GROUP + j) * HD:(g * A_GROUP + j + 1) * HD], q0, tq)
              for j in range(A_GROUP)]
        q = jnp.concatenate(qs, axis=0).astype(BF16)
        ks = ka_s[pl.ds(start, wlen), g * HD:(g + 1) * HD]
        vs = va_s[pl.ds(start, wlen), g * HD:(g + 1) * HD]
        ls = _dot_nt(q, ks) * scale
        if window:
            rows = lax.broadcasted_iota(jnp.int32, ls.shape, 0)
            cols = lax.broadcasted_iota(jnp.int32, ls.shape, 1)
            qpos = q0 + rows % tq
            kpos = start + cols
            ls = jnp.where(jnp.abs(kpos - qpos) <= WINDOW, ls, NEG)
        sink_col = jnp.concatenate(
            [jnp.full((tq, 1), sink_ref[g, j], F32) for j in range(A_GROUP)], axis=0)
        m = jnp.maximum(jnp.max(ls, axis=-1, keepdims=True), sink_col)
        if n_cache:
            lc = _dot_nt(q, cka_s[:, g * HD:(g + 1) * HD]) * scale
            m = jnp.maximum(m, jnp.max(lc, axis=-1, keepdims=True))
        ps = jnp.exp(ls - m)
        den = jnp.sum(ps, axis=-1, keepdims=True) + jnp.exp(sink_col - m)
        o = _dot(ps.astype(BF16), vs)
        if n_cache:
            pc = jnp.exp(lc - m)
            den = den + jnp.sum(pc, axis=-1, keepdims=True)
            o = o + _dot(pc.astype(BF16), cva_s[:, g * HD:(g + 1) * HD])
        o = o / den
        for j in range(A_GROUP):
            hcol = (g * A_GROUP + j) * HD
            o_ref[:, hcol:hcol + HD] = o[j * tq:(j + 1) * tq].astype(BF16)

    lam = (jnp.exp(jnp.sum(lam_ref[0:1, :] * lam_ref[1:2, :], axis=-1, keepdims=True))
           - jnp.exp(jnp.sum(lam_ref[2:3, :] * lam_ref[3:4, :], axis=-1, keepdims=True)) + lam_init)
    for h in range(B_HEADS):
        outs = []
        for w in range(2):
            col = (2 * h + w) * HD
            q = rope_fn(qb_ref[:, col:col + HD], q0, tq).astype(BF16)
            ls = _dot_nt(q, kb_s[:, col:col + HD]) * scale
            m = jnp.max(ls, axis=-1, keepdims=True)
            if n_cache:
                lc = _dot_nt(q, ckb_s[:, col:col + HD]) * scale
                m = jnp.maximum(m, jnp.max(lc, axis=-1, keepdims=True))
            ps = jnp.exp(ls - m)
            den = jnp.sum(ps, axis=-1, keepdims=True)
            o = _dot(ps.astype(BF16), vb_s[:, h * B_VD:(h + 1) * B_VD])
            if n_cache:
                pc = jnp.exp(lc - m)
                den = den + jnp.sum(pc, axis=-1, keepdims=True)
                o = o + _dot(pc.astype(BF16), cvb_s[:, h * B_VD:(h + 1) * B_VD])
            outs.append(o / den)
        ob = outs[0] - lam * outs[1]
        y = ob * lax.rsqrt(jnp.mean(ob * ob, axis=-1, keepdims=True) + EPS)
        y = y * sub_ref[...] * (1.0 - lam_init)
        o_ref[:, A_Q + h * B_VD:A_Q + (h + 1) * B_VD] = y.astype(BF16)


def _attention(p, row0, nb, L, tq, sink, lam_vecs, subln_g, lam_init, name, caches=None,
               rope_tabs=None):
    assert row0 % L == 0 and L % tq == 0
    nq = L // tq
    qb0 = row0 // tq
    kb0 = row0 // L
    n_cache = 0 if caches is None else caches[0].shape[1]
    rope = rope_tabs is not None
    in_specs = [
        pl.BlockSpec(memory_space=pltpu.SMEM),
        pl.BlockSpec((4, HD), lambda b, i: (0, 0)),
        pl.BlockSpec((1, B_VD), lambda b, i: (0, 0)),
    ]
    args = [sink, lam_vecs, subln_g.reshape(1, B_VD)]
    if rope:
        in_specs += [pl.BlockSpec((L, HD), lambda b, i: (0, 0))] * 3
        args += list(rope_tabs)
    in_specs += [
        pl.BlockSpec((tq, A_Q), lambda b, i: (qb0 + b * nq + i, 0)),
        pl.BlockSpec((tq, B_QK), lambda b, i: (qb0 + b * nq + i, 1)),
        pl.BlockSpec((L, B_QK), lambda b, i: (kb0 + b, 2)),
        pl.BlockSpec((L, B_V), lambda b, i: (kb0 + b, 3)),
        pl.BlockSpec((L, 2 * A_KV), lambda b, i: (kb0 + b, (A_Q + 2 * B_QK + B_V) // (2 * A_KV))),
    ]
    args += [p] * 5
    scratch = [pltpu.VMEM((L, B_QK), BF16), pltpu.VMEM((L, B_V), BF16),
               pltpu.VMEM((L, A_KV), BF16), pltpu.VMEM((L, A_KV), BF16)]
    if n_cache:
        for cch in caches:
            in_specs.append(pl.BlockSpec((None,) + cch.shape[1:], lambda b, i: (b, 0, 0)))
            scratch.append(pltpu.VMEM(cch.shape[1:], BF16))
        args += list(caches)
    return pl.pallas_call(
        functools.partial(_attn_kernel, L=L, tq=tq, n_cache=n_cache, rope=rope,
                          window=rope, lam_init=lam_init),
        grid=(nb, nq),
        in_specs=in_specs,
        out_specs=pl.BlockSpec((tq, ATTN_MIX), lambda b, i: (b * nq + i, 0)),
        out_shape=jax.ShapeDtypeStruct((nb * L, ATTN_MIX), BF16),
        scratch_shapes=scratch,
        compiler_params=_cparams("parallel", "arbitrary"),
        name=name,
    )(*args)


def _rope_tables(L):
    rows = L // GRID_W
    row = jnp.repeat(jnp.arange(rows), GRID_W).astype(F32)
    col = jnp.tile(jnp.arange(GRID_W), rows).astype(F32)
    half = HD // 2
    inv = ROPE_BASE ** (-jnp.arange(0, half, 2, dtype=F32) / half)
    ar = row[:, None] * inv
    ac = col[:, None] * inv
    ang = jnp.concatenate([ar, ar, ac, ac], axis=-1)
    cos, sin = jnp.cos(ang), jnp.sin(ang)
    first = (jnp.arange(HD) % (HD // 2)) < (HD // 4)
    return cos, jnp.where(first, -sin, 0.0), jnp.where(first, 0.0, sin)


CONV_ROWS = 256
CONV_HALO = SUBLANES


def _conv_kernel(x_ref, prev_ref, next_ref, w_ref, b_ref, o_ref, pad, *, n_ctx, lp, ld):
    r0 = pl.program_id(0) * CONV_ROWS
    in_seq = jnp.where(r0 < n_ctx, r0 % lp, (r0 - n_ctx) % ld)
    seq_len = jnp.where(r0 < n_ctx, lp, ld)
    has_prev = in_seq > 0
    has_next = in_seq + CONV_ROWS < seq_len
    zeros = jnp.zeros((CONV_HALO, x_ref.shape[1]), F32)
    pad[0:CONV_HALO, :] = jnp.where(has_prev, prev_ref[...], zeros)
    pad[CONV_HALO:CONV_HALO + CONV_ROWS, :] = x_ref[...]
    pad[CONV_HALO + CONV_ROWS:, :] = jnp.where(has_next, next_ref[...], zeros)
    acc = b_ref[...] + x_ref[...] * w_ref[CONV_K // 2:CONV_K // 2 + 1, :]
    for k in range(CONV_K):
        d = k - CONV_K // 2
        if d != 0:
            acc = acc + pad[CONV_HALO + d:CONV_HALO + d + CONV_ROWS, :] * w_ref[k:k + 1, :]
    o_ref[...] = acc * _sigmoid(acc)


def _conv_silu(zx, conv_w, conv_b, n_ctx, lp, ld):
    n = zx.shape[0]
    tc = 1024
    c0 = D_INNER // tc
    hb = CONV_ROWS // CONV_HALO
    nblk = n // CONV_HALO
    return pl.pallas_call(
        functools.partial(_conv_kernel, n_ctx=n_ctx, lp=lp, ld=ld),
        grid=(n // CONV_ROWS, CONV_CH // tc),
        in_specs=[
            pl.BlockSpec((CONV_ROWS, tc), lambda i, j: (i, c0 + j)),
            pl.BlockSpec((CONV_HALO, tc), lambda i, j: (jnp.maximum(i * hb - 1, 0), c0 + j)),
            pl.BlockSpec((CONV_HALO, tc), lambda i, j: (jnp.minimum((i + 1) * hb, nblk - 1), c0 + j)),
            pl.BlockSpec((CONV_K, tc), lambda i, j: (0, j)),
            pl.BlockSpec((1, tc), lambda i, j: (0, j)),
        ],
        out_specs=pl.BlockSpec((CONV_ROWS, tc), lambda i, j: (i, j)),
        out_shape=jax.ShapeDtypeStruct((n, CONV_CH), F32),
        scratch_shapes=[pltpu.VMEM((CONV_ROWS + 2 * CONV_HALO, tc), F32)],
        compiler_params=_cparams("parallel", "parallel"),
        name="ssd_conv",
    )(zx, zx, zx, conv_w, conv_b.reshape(1, CONV_CH))


def _chunk_info(gc, nseq_c, ncc, ncl):
    n_c = nseq_c * ncc
    is_l = gc >= n_c
    seq = jnp.where(is_l, nseq_c + (gc - n_c) // ncl, gc // ncc)
    c = jnp.where(is_l, (gc - n_c) % ncl, gc % ncc)
    nc = jnp.where(is_l, ncl, ncc)
    return seq, c, nc, gc - c, is_l


def _split3(v):
    hi = v.astype(BF16)
    r1 = v - hi.astype(F32)
    mid = r1.astype(BF16)
    lo = (r1 - mid.astype(F32)).astype(BF16)
    return jnp.concatenate([hi, mid, lo], axis=1)


def _ssd_kernel(x_ref, b_ref, c_ref, dt_ref, dtb_ref, alog_ref, h0_ref, e3_ref, y_ref, hfin_ref, h_scr,
                *, nseq_c, ncc, ncl):
    fwd = pl.program_id(0) == 0
    _, c, nc, _, is_l = _chunk_info(pl.program_id(1), nseq_c, ncc, ncl)

    @pl.when(c == 0)
    def _():
        h_scr[...] = jnp.where(is_l, h0_ref[...], jnp.zeros_like(h_scr))

    r = lax.broadcasted_iota(jnp.int32, (CHUNK, CHUNK), 0)
    s = lax.broadcasted_iota(jnp.int32, (CHUNK, CHUNK), 1)
    incl = jnp.where(fwd, r - s, s - r) >= 0
    raw = dt_ref[...] + dtb_ref[...]
    dt = jnp.maximum(raw, 0.0) + jnp.log(1.0 + jnp.exp(-jnp.abs(raw)))
    dta = dt * (-jnp.exp(alog_ref[...]))
    ac = _dot(jnp.where(incl, 1.0, 0.0), dta, precision=HIGHEST)
    ac_last = jnp.where(fwd, ac[CHUNK - 1:CHUNK, :], ac[0:1, :])
    eac = jnp.exp(ac)
    dtd = dt * jnp.exp(ac_last - ac)
    cdec = jnp.exp(ac_last)
    rows_t = jnp.transpose(jnp.concatenate([ac, dt], axis=1))
    lane = lax.broadcasted_iota(jnp.int32, (CHUNK, 2 * SSD_HEADDIM), 1)
    first = lane < SSD_HEADDIM
    pw = 2 * SSD_HEADDIM
    gw = SSD_HPG * SSD_HEADDIM
    dtd3 = _split3(dtd)
    eac3 = _split3(eac)

    for g in range(SSD_GROUPS):
        cols = slice(g * gw, (g + 1) * gw)
        dtd_e = _dot(dtd3, e3_ref[:, cols])
        eac_e = _dot(eac3, e3_ref[:, cols])
        bg = b_ref[:, g * SSD_STATE:(g + 1) * SSD_STATE].astype(BF16)
        cg = c_ref[:, g * SSD_STATE:(g + 1) * SSD_STATE].astype(BF16)
        scores = _dot_nt(cg, bg)
        xg = x_ref[:, cols]
        h_in = h_scr[cols, :]
        y_off = _dot_nt(cg, h_in.astype(BF16)) * eac_e
        st = _dot_tn((xg * dtd_e).astype(BF16), bg)
        yd = []
        for pp in range(SSD_HPG // 2):
            h0 = g * SSD_HPG + 2 * pp
            xp = xg[:, pp * pw:(pp + 1) * pw].astype(BF16)
            two = []
            for h in (h0, h0 + 1):
                decay = jnp.exp(jnp.where(incl, ac[:, h:h + 1] - rows_t[h:h + 1, :], NEG))
                mm = (scores * decay * rows_t[SSD_HEADS + h:SSD_HEADS + h + 1, :]).astype(BF16)
                two.append(_dot(mm, xp))
            yd.append(jnp.where(first, two[0], two[1]))
        y_ref[:, cols] = jnp.concatenate(yd, axis=1) + y_off
        for hh in range(SSD_HPG):
            h = g * SSD_HPG + hh
            lo = hh * SSD_HEADDIM
            h_scr[h * SSD_HEADDIM:(h + 1) * SSD_HEADDIM, :] = (
                cdec[:, h:h + 1] * h_in[lo:lo + SSD_HEADDIM, :] + st[lo:lo + SSD_HEADDIM, :])

    @pl.when(jnp.logical_and(c == nc - 1, jnp.logical_not(is_l)))
    def _():
        hfin_ref[...] = h_scr[...]


def _ssd_scan(xc, dt2, dt_bias, a_log, h0, nseq_c, lp, ld):
    n = xc.shape[0]
    ncc, ncl = lp // CHUNK, ld // CHUNK
    nchunks = n // CHUNK
    bd = h0.shape[1]
    sw = SSD_GROUPS * SSD_STATE
    info = functools.partial(_chunk_info, nseq_c=nseq_c, ncc=ncc, ncl=ncl)

    def rowblk(d, gc):
        _, c, nc, start, _ = info(gc)
        return start + jnp.where(d == 0, c, nc - 1 - c)

    def seq_l(gc):
        return jnp.clip(info(gc)[0] - nseq_c, 0, bd - 1)

    def seq_c(gc):
        return jnp.minimum(info(gc)[0], nseq_c - 1)

    head_of_col = jnp.arange(D_INNER) // SSD_HEADDIM
    expand3 = (jnp.arange(3 * SSD_HEADS)[:, None] % SSD_HEADS == head_of_col[None, :]).astype(BF16)

    return pl.pallas_call(
        functools.partial(_ssd_kernel, nseq_c=nseq_c, ncc=ncc, ncl=ncl),
        grid=(2, nchunks),
        in_specs=[
            pl.BlockSpec((CHUNK, D_INNER), lambda d, gc: (rowblk(d, gc), 0)),
            pl.BlockSpec((CHUNK, sw), lambda d, gc: (rowblk(d, gc), D_INNER // sw)),
            pl.BlockSpec((CHUNK, sw), lambda d, gc: (rowblk(d, gc), D_INNER // sw + 1)),
            pl.BlockSpec((None, CHUNK, SSD_HEADS), lambda d, gc: (d, rowblk(d, gc), 0)),
            pl.BlockSpec((None, 1, SSD_HEADS), lambda d, gc: (d, 0, 0)),
            pl.BlockSpec((None, 1, SSD_HEADS), lambda d, gc: (d, 0, 0)),
            pl.BlockSpec((None, None, D_INNER, SSD_STATE), lambda d, gc: (d, seq_l(gc), 0, 0)),
            pl.BlockSpec((3 * SSD_HEADS, D_INNER), lambda d, gc: (0, 0)),
        ],
        out_specs=[
            pl.BlockSpec((None, CHUNK, D_INNER), lambda d, gc: (d, rowblk(d, gc), 0)),
            pl.BlockSpec((None, None, D_INNER, SSD_STATE), lambda d, gc: (d, seq_c(gc), 0, 0)),
        ],
        out_shape=[
            jax.ShapeDtypeStruct((2, n, D_INNER), F32),
            jax.ShapeDtypeStruct((2, nseq_c, D_INNER, SSD_STATE), F32),
        ],
        scratch_shapes=[pltpu.VMEM((D_INNER, SSD_STATE), F32)],
        compiler_params=_cparams("arbitrary", "arbitrary"),
        name="ssd_scan",
    )(xc, xc, xc, dt2, dt_bias.reshape(2, 1, SSD_HEADS), a_log.reshape(2, 1, SSD_HEADS), h0, expand3)


def _ssd_gate_kernel(yf_ref, yb_ref, xs_ref, z_ref, d_ref, g_ref, o_ref):
    z = z_ref[...]
    y = (yf_ref[...] + yb_ref[...] + d_ref[...] * xs_ref[...]) * (z * _sigmoid(z))
    gs = D_INNER // SSD_GROUPS
    for g in range(SSD_GROUPS):
        yg = y[:, g * gs:(g + 1) * gs]
        n = yg * lax.rsqrt(jnp.mean(yg * yg, axis=-1, keepdims=True) + EPS)
        o_ref[:, g * gs:(g + 1) * gs] = (n * g_ref[:, g * gs:(g + 1) * gs]).astype(BF16)


def _ssd_gate(y2, xc, zx, d_skip, norm_g):
    n = xc.shape[0]
    tm = 128
    d_e = jnp.repeat(d_skip, SSD_HEADDIM).reshape(1, D_INNER)
    return pl.pallas_call(
        _ssd_gate_kernel,
        grid=(n // tm,),
        in_specs=[
            pl.BlockSpec((None, tm, D_INNER), lambda i: (0, i, 0)),
            pl.BlockSpec((None, tm, D_INNER), lambda i: (1, i, 0)),
            pl.BlockSpec((tm, D_INNER), lambda i: (i, 0)),
            pl.BlockSpec((tm, D_INNER), lambda i: (i, 0)),
            pl.BlockSpec((1, D_INNER), lambda i: (0, 0)),
            pl.BlockSpec((1, D_INNER), lambda i: (0, 0)),
        ],
        out_specs=pl.BlockSpec((tm, D_INNER), lambda i: (i, 0)),
        out_shape=jax.ShapeDtypeStruct((n, D_INNER), BF16),
        compiler_params=_cparams("parallel"),
        name="ssd_gate",
    )(y2, y2, xc, zx, d_e, norm_g.reshape(1, D_INNER))


def _oddeven_merge(lo, hi, r):
    step = r * 2
    if step < hi - lo:
        yield from _oddeven_merge(lo, hi, step)
        yield from _oddeven_merge(lo + r, hi, step)
        for i in range(lo + r, hi - r, step):
            yield (i, i + r)
    else:
        yield (lo, lo + r)


def _oddeven_sort(lo, hi):
    if hi - lo >= 1:
        mid = lo + (hi - lo) // 2
        yield from _oddeven_sort(lo, mid)
        yield from _oddeven_sort(mid + 1, hi)
        yield from _oddeven_merge(lo, hi, 1)


def _cmpx(v, i, j):
    a, b = v[i], v[j]
    if b is None:
        return
    if a is None:
        v[i], v[j] = b, None
        return
    v[i], v[j] = jnp.maximum(a, b), jnp.minimum(a, b)


def _top_across_sublanes(v):
    n = len(v)
    for i, j in _oddeven_sort(0, n - 1):
        _cmpx(v, i, j)
    for shift in (4, 2, 1):
        rolled = [None if a is None else pltpu.roll(a, shift, 0) for a in v]
        c = []
        for i in range(n):
            a, b = v[i], rolled[n - 1 - i]
            c.append(a if b is None else (b if a is None else jnp.maximum(a, b)))
        stride = n // 2
        while stride >= 1:
            for i in range(n):
                if i & stride == 0:
                    _cmpx(c, i, i + stride)
            stride //= 2
        v = c
    return v


def _topk_kernel(q_ref, sk_ref, s1_ref, s2_ref, aux_ref, s_scr, sv_scr, aux_scr):
    tk = q_ref.shape[0]
    ng = tk // LANES
    nhc = 2 * PEER_HEADS
    for hc in range(nhc):
        qh = q_ref[:, hc * PEER_DHALF:(hc + 1) * PEER_DHALF].astype(BF16)
        s_t = _dot_nt(sk_ref[hc].astype(BF16), qh)
        if hc % 2 == 0:
            s1_ref[hc // 2] = s_t
        else:
            s2_ref[hc // 2] = s_t
        for g in range(ng):
            s_scr[g * nhc + hc] = s_t[:, g * LANES:(g + 1) * LANES]

    def stage1(k, carry):
        x = s_scr[k]
        v = _top_across_sublanes([x[SUBLANES * i:SUBLANES * (i + 1), :] for i in range(PEER_TOPK)])
        sv_scr[k] = jnp.concatenate([a[0:1, :] for a in v], axis=0)
        return carry

    lax.fori_loop(0, ng * nhc, stage1, 0)

    def stage2(k, carry):
        g = k // PEER_HEADS
        h = k % PEER_HEADS
        sv1 = sv_scr[g * nhc + 2 * h]
        sv2 = sv_scr[g * nhc + 2 * h + 1]
        b1 = [jnp.broadcast_to(sv1[i:i + 1, :], (SUBLANES, LANES)) for i in range(SUBLANES)]
        b20 = jnp.broadcast_to(sv2[0:1, :], (SUBLANES, LANES))
        lo2, hi2, hi1 = sv2[0:SUBLANES, :], sv2[SUBLANES:, :], sv1[SUBLANES:, :]
        cands = [b1[0] + lo2, b1[0] + hi2] + [b1[i] + lo2 for i in range(1, SUBLANES)] + [hi1 + b20]
        t = _top_across_sublanes(cands + [None] * (PEER_TOPK - len(cands)))
        z = jnp.zeros_like(t[0])
        for a in t:
            z = z + jnp.exp(a - t[0])
        aux_scr[k] = jnp.concatenate(
            [t[PEER_TOPK - 1][0:1, :],
             b1[0][0:1, :],
             b20[0:1, :],
             z[0:1, :]], axis=0)
        return carry

    lax.fori_loop(0, ng * PEER_HEADS, stage2, 0)
    for g in range(ng):
        for h in range(PEER_HEADS):
            aux_ref[h, :, g * LANES:(g + 1) * LANES] = aux_scr[g * PEER_HEADS + h]


def _peer_topk(q, sub_keys):
    n = q.shape[0]
    tk = 2 * LANES
    ng = tk // LANES
    sk = sub_keys.reshape(2 * PEER_HEADS, PEER_KEYS, PEER_DHALF)
    return pl.pallas_call(
        _topk_kernel,
        grid=(n // tk,),
        in_specs=[
            pl.BlockSpec((tk, PEER_HEADS * PEER_DKEY), lambda i: (i, 0)),
            pl.BlockSpec(sk.shape, lambda i: (0, 0, 0)),
        ],
        out_specs=[
            pl.BlockSpec((PEER_HEADS, PEER_KEYS, tk), lambda i: (0, 0, i)),
            pl.BlockSpec((PEER_HEADS, PEER_KEYS, tk), lambda i: (0, 0, i)),
            pl.BlockSpec((PEER_HEADS, 4, tk), lambda i: (0, 0, i)),
        ],
        out_shape=[
            jax.ShapeDtypeStruct((PEER_HEADS, PEER_KEYS, n), F32),
            jax.ShapeDtypeStruct((PEER_HEADS, PEER_KEYS, n), F32),
            jax.ShapeDtypeStruct((PEER_HEADS, 4, n), F32),
        ],
        scratch_shapes=[
            pltpu.VMEM((ng * 2 * PEER_HEADS, PEER_KEYS, LANES), F32),
            pltpu.VMEM((ng * 2 * PEER_HEADS, PEER_TOPK, LANES), F32),
            pltpu.VMEM((ng * PEER_HEADS, 4, LANES), F32),
        ],
        compiler_params=_cparams("parallel"),
        name="peer_topk",
    )(q, sk)


_GELU_K0 = -2.0 * math.sqrt(2.0 / math.pi) * math.log2(math.e)
_GELU_K1 = 0.044715 * _GELU_K0


def _gelu_tanh(x):
    return x / (1.0 + jnp.exp2(x * (_GELU_K0 + _GELU_K1 * (x * x))))


PEER_ROWS = 64


def _peer_kernel(hb_ref, u_ref, v_ref, s1_ref, s2_ref, aux_ref, x_ref, gate_ref, o_ref,
                 acc, st, act, a2, m1z):
    e = pl.program_id(1)
    te, tm = st.shape

    @pl.when(e == 0)
    def _():
        acc[...] = jnp.zeros_like(acc)
        for h in range(PEER_HEADS):
            a2[h] = jnp.exp(s2_ref[h] - aux_ref[h, 2:3, :])
            m1z[h:h + 1, :] = aux_ref[h, 1:2, :] + jnp.log(aux_ref[h, 3:4, :])

    st[...] = _dot_nt(u_ref[...], hb_ref[...])
    for a in range(te // PEER_KEYS):
        for tc in range(tm // LANES):
            ln = slice(tc * LANES, (tc + 1) * LANES)
            s1row = [s1_ref[h, a:a + 1, ln] for h in range(PEER_HEADS)]
            a1row = [jnp.exp(s1row[h] - m1z[h:h + 1, ln]) for h in range(PEER_HEADS)]
            for k0 in range(0, PEER_KEYS, PEER_ROWS):
                kr = slice(k0, k0 + PEER_ROWS)
                w = None
                for h in range(PEER_HEADS):
                    sel = (s1row[h] + s2_ref[h, kr, ln]) >= aux_ref[h, 0:1, ln]
                    t = jnp.where(sel, a2[h, kr, ln], 0.0) * a1row[h]
                    w = t if w is None else w + t
                rows = slice(a * PEER_KEYS + k0, a * PEER_KEYS + k0 + PEER_ROWS)
                act[rows, ln] = (_gelu_tanh(st[rows, ln]) * w).astype(BF16)
    acc[...] += _dot_tn(act[...], v_ref[...])

    @pl.when(e == pl.num_programs(1) - 1)
    def _():
        o_ref[...] = x_ref[...] + gate_ref[0] * acc[...]


def _peer_dense(hb, u_b, v_b, s1, s2, aux, x, mod, gate_k, n_ctx, ld, tm=512, te=1024):
    n, d = x.shape
    ne = PEER_EXPERTS // te
    cond = functools.partial(_cond_of_tile, tm=tm, n_ctx=n_ctx, ld=ld)
    once = dict(pipeline_mode=pl.Buffered(1))
    return pl.pallas_call(
        _peer_kernel,
        grid=(n // tm, ne),
        in_specs=[
            pl.BlockSpec((tm, d), lambda i, e: (i, 0), **once),
            pl.BlockSpec((te, d), lambda i, e: (e, 0)),
            pl.BlockSpec((te, d), lambda i, e: (e, 0)),
            pl.BlockSpec((PEER_HEADS, te // PEER_KEYS, tm), lambda i, e: (0, e, i)),
            pl.BlockSpec((PEER_HEADS, PEER_KEYS, tm), lambda i, e: (0, 0, i), **once),
            pl.BlockSpec((PEER_HEADS, 4, tm), lambda i, e: (0, 0, i)),
            pl.BlockSpec((tm, d), lambda i, e: (i, 0), **once),
            pl.BlockSpec((1, 1, d), lambda i, e: (cond(i) * 6 + gate_k, 0, 0)),
        ],
        out_specs=pl.BlockSpec((tm, d), lambda i, e: (i, 0)),
        out_shape=jax.ShapeDtypeStruct((n, d), F32),
        scratch_shapes=[
            pltpu.VMEM((tm, d), F32),
            pltpu.VMEM((te, tm), F32),
            pltpu.VMEM((te, tm), BF16),
            pltpu.VMEM((PEER_HEADS, PEER_KEYS, tm), F32),
            pltpu.VMEM((PEER_HEADS, tm), F32),
        ],
        compiler_params=_cparams("parallel", "arbitrary"),
        name="peer_dense",
    )(hb, u_b, v_b, s1, s2, aux, x, mod)


def _peer_layer(x, mod, norm_g, w_q, sub_keys, u, v, n_ctx, ld, tm):
    q, hb = _proj(x, w_q.astype(BF16), tm=tm, tn=512, n_ctx=n_ctx, ld=ld, name="peer_q", mod=mod,
                  norm_g=norm_g, shift_k=3, scale_k=4, emit_h=True)
    s1, s2, aux = _peer_topk(q, sub_keys)
    return _peer_dense(hb, u.astype(BF16), v.astype(BF16), s1, s2, aux, x, mod, 5, n_ctx, ld)


def _rmsnorm_kernel(x_ref, g_ref, o_ref):
    x = x_ref[...]
    o_ref[...] = x * lax.rsqrt(jnp.mean(x * x, axis=-1, keepdims=True) + EPS) * g_ref[...]


def _final_norm(x, g):
    n, d = x.shape
    tm = 512
    return pl.pallas_call(
        _rmsnorm_kernel,
        grid=(n // tm,),
        in_specs=[pl.BlockSpec((tm, d), lambda i: (i, 0)), pl.BlockSpec((1, d), lambda i: (0, 0))],
        out_specs=pl.BlockSpec((tm, d), lambda i: (i, 0)),
        out_shape=jax.ShapeDtypeStruct((n, d), F32),
        compiler_params=_cparams("parallel"),
        name="final_norm",
    )(x, g.reshape(1, d))


def _diff_lambda_init(layer):
    return 0.8 - 0.6 * math.exp(-0.3 * layer)


def kernel(x_prompt, x_sample, cache_a_k, cache_a_v, cache_b_k, cache_b_v, state_ssd_fwd, state_ssd_bwd, c, c_ctx, w_mod, b_mod, norm_g, attn_w_in, attn_w_out, attn_sink, diff_lam, diff_subln_g, ssd_w_in, ssd_conv_w, ssd_conv_b, ssd_dt_bias, ssd_a_log, ssd_d, ssd_norm_g, ssd_w_out, peer_w_q, peer_sub_keys, peer_u, peer_v, final_g):
    bp, lp, d = x_prompt.shape
    bd, ld, _ = x_sample.shape
    past = cache_a_k.shape[2]
    n_ctx, n_lat = bp * lp, bd * ld
    tm = 512
    assert n_ctx % ld == 0 and n_ctx % tm == 0 and ld % tm == 0
    assert lp % CONV_ROWS == 0 and ld % CONV_ROWS == 0

    x = jnp.concatenate([x_prompt.reshape(n_ctx, d), x_sample.reshape(n_lat, d)], axis=0)

    r = -(-(bd + 1) // SUBLANES) * SUBLANES
    cond = jnp.concatenate([c_ctx[None], c, jnp.zeros((r - bd - 1, d), F32)], axis=0)
    mod_all = _modulation(cond, w_mod, b_mod).reshape(DEPTH, r * 6, 1, d)

    outs = {}
    for l in range(DEPTH):
        i = l // 2
        mod = mod_all[l]
        if l % 2 == 0:
            lam_init = _diff_lambda_init(l)
            w_in = attn_w_in[i]
            sp = (0, A_Q, A_Q + A_KV, A_Q + 2 * A_KV, A_Q + 2 * A_KV + B_QK, A_Q + 2 * A_KV + 2 * B_QK, ATTN_IN)
            aq, ak, av, bq, bk, bv = (w_in[:, sp[k]:sp[k + 1]] for k in range(6))
            w_perm = jnp.concatenate([aq, bq, bk, bv, ak, av], axis=1).astype(BF16)
            p = _proj(x, w_perm, tm=tm, tn=1152, n_ctx=n_ctx, ld=ld, name="attn_in", mod=mod,
                      norm_g=norm_g[l, 0], shift_k=0, scale_k=1)
            mix_c = _attention(p, 0, bp, lp, lp, attn_sink[i], diff_lam[i], diff_subln_g[i], lam_init,
                               "attn_ctx")
            caches = (cache_a_k[:, i].reshape(bd, past, A_KV), cache_a_v[:, i].reshape(bd, past, A_KV),
                      cache_b_k[:, i].reshape(bd, past, B_QK), cache_b_v[:, i].reshape(bd, past, B_V))
            mix_l = _attention(p, n_ctx, bd, ld, WINDOW, attn_sink[i], diff_lam[i], diff_subln_g[i],
                               lam_init, "attn_lat", caches=caches, rope_tabs=_rope_tables(ld))
            x = _proj((mix_c, mix_l), attn_w_out[i].astype(BF16), tm=tm, tn=512, n_ctx=n_ctx, ld=ld,
                      name="attn_out", mod=mod, res=x, gate_k=2)
            c0 = A_Q + 2 * B_QK + B_V
            pc = p[:n_ctx]
            outs.setdefault("ak", []).append(pc[:, c0:c0 + A_KV].reshape(bp, lp, A_KV_HEADS, HD))
            outs.setdefault("av", []).append(pc[:, c0 + A_KV:c0 + 2 * A_KV].reshape(bp, lp, A_KV_HEADS, HD))
            outs.setdefault("bk", []).append(pc[:, A_Q + B_QK:A_Q + 2 * B_QK].reshape(bp, lp, B_HEADS, 2, HD))
            outs.setdefault("bv", []).append(pc[:, A_Q + 2 * B_QK:c0].reshape(bp, lp, B_HEADS, B_VD))
        else:
            w_in = ssd_w_in[i]
            nzx = D_INNER + CONV_CH
            zx = _proj(x, w_in[:, :nzx].astype(BF16), tm=tm, tn=2048, n_ctx=n_ctx, ld=ld, name="ssd_in",
                       mod=mod, norm_g=norm_g[l, 0], shift_k=0, scale_k=1)
            dtr = _proj(x, w_in[:, nzx:].astype(BF16), tm=tm, tn=2 * SSD_HEADS, n_ctx=n_ctx, ld=ld,
                        name="ssd_dt", mod=mod, norm_g=norm_g[l, 0], shift_k=0, scale_k=1)
            dt2 = jnp.transpose(dtr.reshape(-1, 2, SSD_HEADS), (1, 0, 2))
            xc = _conv_silu(zx, ssd_conv_w[i], ssd_conv_b[i], n_ctx, lp, ld)
            h0 = jnp.stack([state_ssd_fwd[:, i], state_ssd_bwd[:, i]]).reshape(2, bd, D_INNER, SSD_STATE)
            y2, hfin = _ssd_scan(xc, dt2, ssd_dt_bias[i], ssd_a_log[i], h0, bp, lp, ld)
            hg = _ssd_gate(y2, xc, zx, ssd_d[i], ssd_norm_g[i])
            x = _proj(hg, ssd_w_out[i].astype(BF16), tm=tm, tn=512, n_ctx=n_ctx, ld=ld, name="ssd_out",
                      mod=mod, res=x, gate_k=2)
            hs = hfin.reshape(2, bp, SSD_HEADS, SSD_HEADDIM, SSD_STATE)
            outs.setdefault("sf", []).append(hs[0])
            outs.setdefault("sb", []).append(hs[1])
        x = _peer_layer(x, mod, norm_g[l, 1], peer_w_q[l], peer_sub_keys[l], peer_u[l], peer_v[l],
                        n_ctx, ld, tm)

    y = _final_norm(x, final_g)
    y_prompt = y[:n_ctx].reshape(bp, lp, d)
    y_sample = y[n_ctx:].reshape(bd, ld, d)
    return (y_prompt, y_sample, jnp.stack(outs["ak"], axis=1), jnp.stack(outs["av"], axis=1),
            jnp.stack(outs["bk"], axis=1), jnp.stack(outs["bv"], axis=1),
            jnp.stack(outs["sf"], axis=1), jnp.stack(outs["sb"], axis=1))
```

```python
import functools
import math

import jax
import jax.numpy as jnp
from jax import lax
from jax.experimental import pallas as pl
from jax.experimental.pallas import tpu as pltpu

F32 = jnp.float32
BF16 = jnp.bfloat16

D_MODEL = 2048
DEPTH = 2
GRID_W = 64
EPS = 1e-6
NEG = -1e30

HD = 128
A_HEADS = 8
A_KV_HEADS = 2
A_GROUP = A_HEADS // A_KV_HEADS
WINDOW = 128
B_HEADS = 4
B_VD = 2 * HD
A_Q = A_HEADS * HD
A_KV = A_KV_HEADS * HD
B_QK = B_HEADS * 2 * HD
B_V = B_HEADS * B_VD
ATTN_IN = A_Q + 2 * A_KV + 2 * B_QK + B_V
ATTN_MIX = A_Q + B_V
ROPE_BASE = 10000.0

D_INNER = 2 * D_MODEL
SSD_HEADDIM = 64
SSD_HEADS = D_INNER // SSD_HEADDIM
SSD_GROUPS = 8
SSD_HPG = SSD_HEADS // SSD_GROUPS
SSD_STATE = 128
CONV_K = 5
CONV_CH = D_INNER + 2 * SSD_GROUPS * SSD_STATE
CHUNK = 128

PEER_HEADS = 8
PEER_KEYS = 128
PEER_EXPERTS = PEER_KEYS * PEER_KEYS
PEER_TOPK = 16
PEER_DKEY = 256
PEER_DHALF = PEER_DKEY // 2

V7X_VMEM_LIMIT_BYTES = 56 * 1024 * 1024
LANES = 128
SUBLANES = 8

HIGHEST = lax.Precision.HIGHEST


def _cparams(*sem):
    return pltpu.CompilerParams(dimension_semantics=sem, vmem_limit_bytes=V7X_VMEM_LIMIT_BYTES)


def _dot(a, b, precision=None):
    return jnp.dot(a, b, preferred_element_type=F32, precision=precision)


def _dot_nt(a, b):
    return lax.dot_general(a, b, (((1,), (1,)), ((), ())), preferred_element_type=F32)


def _dot_tn(a, b):
    return lax.dot_general(a, b, (((0,), (0,)), ((), ())), preferred_element_type=F32)


def _sigmoid(x):
    return 1.0 / (1.0 + jnp.exp(-x))


def _cond_of_tile(i, tm, n_ctx, ld):
    r = i * tm
    return jnp.where(r < n_ctx, 0, 1 + (r - n_ctx) // ld)


def _mod_kernel(c_ref, w_ref, b_ref, o_ref):
    c = c_ref[...]
    s = c * _sigmoid(c)
    o_ref[...] = _dot(s.astype(BF16), w_ref[...].astype(BF16)) + b_ref[...]


def _modulation(cond, w_mod, b_mod):
    depth, d, n6 = w_mod.shape
    r = cond.shape[0]
    tn = 1024
    return pl.pallas_call(
        _mod_kernel,
        grid=(depth, n6 // tn),
        in_specs=[
            pl.BlockSpec((r, d), lambda l, j: (0, 0)),
            pl.BlockSpec((None, d, tn), lambda l, j: (l, 0, j)),
            pl.BlockSpec((None, 1, tn), lambda l, j: (l, 0, j)),
        ],
        out_specs=pl.BlockSpec((None, r, tn), lambda l, j: (l, 0, j)),
        out_shape=jax.ShapeDtypeStruct((depth, r, n6), F32),
        compiler_params=_cparams("parallel", "parallel"),
        name="modulation",
    )(cond, w_mod, b_mod.reshape(depth, 1, n6))


def _proj_kernel(*refs, adaln, emit_h, residual, split_tiles):
    it = iter(refs)
    x_ref = next(it)
    if split_tiles:
        x2_ref = next(it)
    if adaln:
        g_ref, sh_ref, sc_ref = next(it), next(it), next(it)
    w_ref = next(it)
    if residual:
        r_ref, gate_ref = next(it), next(it)
    o_ref = next(it)
    if emit_h:
        h_ref = next(it)
    if adaln:
        hs_ref = next(it)
        j = pl.program_id(1)

        @pl.when(j == 0)
        def _():
            x = x_ref[...]
            y = x * lax.rsqrt(jnp.mean(x * x, axis=-1, keepdims=True) + EPS)
            y = y * g_ref[...]
            y = y * (1.0 + sc_ref[0]) + sh_ref[0]
            hb = y.astype(BF16)
            hs_ref[...] = hb
            if emit_h:
                h_ref[...] = hb

        h = hs_ref[...]
    elif split_tiles:
        h = jnp.where(pl.program_id(0) < split_tiles, x_ref[...], x2_ref[...])
    else:
        h = x_ref[...]
    acc = _dot(h, w_ref[...])
    if residual:
        o_ref[...] = r_ref[...] + gate_ref[0] * acc
    else:
        o_ref[...] = acc


def _proj(x, w, *, tm, tn, n_ctx, ld, name, mod=None, norm_g=None, shift_k=None, scale_k=None,
          emit_h=False, res=None, gate_k=None):
    split = isinstance(x, tuple)
    if split:
        x, x2 = x
        split_tiles = x.shape[0] // tm
        n, k = x.shape[0] + x2.shape[0], x.shape[1]
    else:
        split_tiles = 0
        n, k = x.shape
    m = w.shape[1]
    adaln = norm_g is not None
    residual = res is not None
    cond = functools.partial(_cond_of_tile, tm=tm, n_ctx=n_ctx, ld=ld)
    if split:
        in_specs = [pl.BlockSpec((tm, k), lambda i, j: (jnp.minimum(i, split_tiles - 1), 0)),
                    pl.BlockSpec((tm, k), lambda i, j: (jnp.maximum(i - split_tiles, 0), 0))]
        args = [x, x2]
    else:
        in_specs = [pl.BlockSpec((tm, k), lambda i, j: (i, 0))]
        args = [x]
    if adaln:
        in_specs += [
            pl.BlockSpec((1, k), lambda i, j: (0, 0)),
            pl.BlockSpec((1, 1, k), lambda i, j: (cond(i) * 6 + shift_k, 0, 0)),
            pl.BlockSpec((1, 1, k), lambda i, j: (cond(i) * 6 + scale_k, 0, 0)),
        ]
        args += [norm_g.reshape(1, k), mod, mod]
    w_mode = dict(pipeline_mode=pl.Buffered(1)) if tn == m else {}
    in_specs.append(pl.BlockSpec((k, tn), lambda i, j: (0, j), **w_mode))
    args.append(w)
    if residual:
        in_specs += [
            pl.BlockSpec((tm, tn), lambda i, j: (i, j)),
            pl.BlockSpec((1, 1, tn), lambda i, j: (cond(i) * 6 + gate_k, 0, j)),
        ]
        args += [res, mod]
    out_specs = [pl.BlockSpec((tm, tn), lambda i, j: (i, j))]
    out_shape = [jax.ShapeDtypeStruct((n, m), F32)]
    if emit_h:
        out_specs.append(pl.BlockSpec((tm, k), lambda i, j: (i, 0)))
        out_shape.append(jax.ShapeDtypeStruct((n, k), BF16))
    scratch = [pltpu.VMEM((tm, k), BF16)] if adaln else []
    outs = pl.pallas_call(
        functools.partial(_proj_kernel, adaln=adaln, emit_h=emit_h, residual=residual,
                          split_tiles=split_tiles),
        grid=(n // tm, m // tn),
        in_specs=in_specs,
        out_specs=out_specs,
        out_shape=out_shape,
        scratch_shapes=scratch,
        compiler_params=_cparams("parallel", "arbitrary"),
        name=name,
    )(*args)
    return outs if emit_h else outs[0]


def _attn_kernel(*refs, L, tq, n_cache, rope, window, lam_init, emit_kv):
    it = iter(refs)
    sink_ref = next(it)
    lam_ref = next(it)
    sub_ref = next(it)
    if rope:
        cos_ref, sa_ref, sb_ref = next(it), next(it), next(it)
    qa_ref, qb_ref, kb_ref, vb_ref, kva_ref = (next(it) for _ in range(5))
    if n_cache:
        cka_ref, cva_ref, ckb_ref, cvb_ref = (next(it) for _ in range(4))
    o_ref = next(it)
    if emit_kv:
        ako_ref, avo_ref, bko_ref, bvo_ref = (next(it) for _ in range(4))
    kb_s, vb_s, ka_s, va_s = (next(it) for _ in range(4))
    if n_cache:
        cka_s, cva_s, ckb_s, cvb_s = (next(it) for _ in range(4))
    i = pl.program_id(1)

    def rope_fn(x, r0, rows):
        if not rope:
            return x
        c = cos_ref[pl.ds(r0, rows), :]
        sa = sa_ref[pl.ds(r0, rows), :]
        sb = sb_ref[pl.ds(r0, rows), :]
        return x * c + pltpu.roll(x, 96, 1) * sa + pltpu.roll(x, 32, 1) * sb

    @pl.when(i == 0)
    def _():
        for g in range(A_KV_HEADS):
            ka_s[:, g * HD:(g + 1) * HD] = rope_fn(kva_ref[:, g * HD:(g + 1) * HD], 0, L).astype(BF16)
        va_s[...] = kva_ref[:, A_KV:2 * A_KV].astype(BF16)
        for c in range(2 * B_HEADS):
            kb_s[:, c * HD:(c + 1) * HD] = rope_fn(kb_ref[:, c * HD:(c + 1) * HD], 0, L).astype(BF16)
        vb_s[...] = vb_ref[...].astype(BF16)
        if emit_kv:
            ako_ref[...] = kva_ref[:, 0:A_KV]
            avo_ref[...] = kva_ref[:, A_KV:2 * A_KV]
            bko_ref[...] = kb_ref[...]
            bvo_ref[...] = vb_ref[...]
        if n_cache:
            cka_s[...] = cka_ref[...].astype(BF16)
            cva_s[...] = cva_ref[...].astype(BF16)
            ckb_s[...] = ckb_ref[...].astype(BF16)
            cvb_s[...] = cvb_ref[...].astype(BF16)

    q0 = pl.multiple_of(i * tq, tq)
    scale = HD ** -0.5

    if window:
        wlen = 3 * WINDOW
        start = pl.multiple_of(jnp.clip((i - 1) * WINDOW, 0, L - wlen), WINDOW)
    else:
        wlen = L
        start = 0
    for g in range(A_KV_HEADS):
        qs = [rope_fn(qa_ref[:, (g * A_GROUP + j) * HD:(g * A_GROUP + j + 1) * HD], q0, tq)
              for j in range(A_GROUP)]
        q = jnp.concatenate(qs, axis=0).astype(BF16)
        ks = ka_s[pl.ds(start, wlen), g * HD:(g + 1) * HD]
        vs = va_s[pl.ds(start, wlen), g * HD:(g + 1) * HD]
        ls = _dot_nt(q, ks) * scale
        if window:
            rows = lax.broadcasted_iota(jnp.int32, ls.shape, 0)
            cols = lax.broadcasted_iota(jnp.int32, ls.shape, 1)
            qpos = q0 + rows % tq
            kpos = start + cols
            ls = jnp.where(jnp.abs(kpos - qpos) <= WINDOW, ls, NEG)
        sink_col = jnp.concatenate(
            [jnp.full((tq, 1), sink_ref[g, j], F32) for j in range(A_GROUP)], axis=0)
        m = jnp.maximum(jnp.max(ls, axis=-1, keepdims=True), sink_col)
        if n_cache:
            lc = _dot_nt(q, cka_s[:, g * HD:(g + 1) * HD]) * scale
            m = jnp.maximum(m, jnp.max(lc, axis=-1, keepdims=True))
        ps = jnp.exp(ls - m)
        den = jnp.sum(ps, axis=-1, keepdims=True) + jnp.exp(sink_col - m)
        o = _dot(ps.astype(BF16), vs)
        if n_cache:
            pc = jnp.exp(lc - m)
            den = den + jnp.sum(pc, axis=-1, keepdims=True)
            o = o + _dot(pc.astype(BF16), cva_s[:, g * HD:(g + 1) * HD])
        o = o / den
        for j in range(A_GROUP):
            hcol = (g * A_GROUP + j) * HD
            o_ref[:, hcol:hcol + HD] = o[j * tq:(j + 1) * tq].astype(BF16)

    lam = (jnp.exp(jnp.sum(lam_ref[0:1, :] * lam_ref[1:2, :], axis=-1, keepdims=True))
           - jnp.exp(jnp.sum(lam_ref[2:3, :] * lam_ref[3:4, :], axis=-1, keepdims=True)) + lam_init)
    for h in range(B_HEADS):
        outs = []
        for w in range(2):
            col = (2 * h + w) * HD
            q = rope_fn(qb_ref[:, col:col + HD], q0, tq).astype(BF16)
            ls = _dot_nt(q, kb_s[:, col:col + HD]) * scale
            m = jnp.max(ls, axis=-1, keepdims=True)
            if n_cache:
                lc = _dot_nt(q, ckb_s[:, col:col + HD]) * scale
                m = jnp.maximum(m, jnp.max(lc, axis=-1, keepdims=True))
            ps = jnp.exp(ls - m)
            den = jnp.sum(ps, axis=-1, keepdims=True)
            o = _dot(ps.astype(BF16), vb_s[:, h * B_VD:(h + 1) * B_VD])
            if n_cache:
                pc = jnp.exp(lc - m)
                den = den + jnp.sum(pc, axis=-1, keepdims=True)
                o = o + _dot(pc.astype(BF16), cvb_s[:, h * B_VD:(h + 1) * B_VD])
            outs.append(o / den)
        ob = outs[0] - lam * outs[1]
        y = ob * lax.rsqrt(jnp.mean(ob * ob, axis=-1, keepdims=True) + EPS)
        y = y * sub_ref[...] * (1.0 - lam_init)
        o_ref[:, A_Q + h * B_VD:A_Q + (h + 1) * B_VD] = y.astype(BF16)


def _attention(p, row0, nb, L, tq, sink, lam_vecs, subln_g, lam_init, name, caches=None,
               rope_tabs=None, emit_kv=False):
    assert row0 % L == 0 and L % tq == 0
    nq = L // tq
    qb0 = row0 // tq
    kb0 = row0 // L
    n_cache = 0 if caches is None else caches[0].shape[1]
    rope = rope_tabs is not None
    in_specs = [
        pl.BlockSpec(memory_space=pltpu.SMEM),
        pl.BlockSpec((4, HD), lambda b, i: (0, 0)),
        pl.BlockSpec((1, B_VD), lambda b, i: (0, 0)),
    ]
    args = [sink, lam_vecs, subln_g.reshape(1, B_VD)]
    if rope:
        in_specs += [pl.BlockSpec((L, HD), lambda b, i: (0, 0))] * 3
        args += list(rope_tabs)
    in_specs += [
        pl.BlockSpec((tq, A_Q), lambda b, i: (qb0 + b * nq + i, 0)),
        pl.BlockSpec((tq, B_QK), lambda b, i: (qb0 + b * nq + i, 1)),
        pl.BlockSpec((L, B_QK), lambda b, i: (kb0 + b, 2)),
        pl.BlockSpec((L, B_V), lambda b, i: (kb0 + b, 3)),
        pl.BlockSpec((L, 2 * A_KV), lambda b, i: (kb0 + b, (A_Q + 2 * B_QK + B_V) // (2 * A_KV))),
    ]
    args += [p] * 5
    scratch = [pltpu.VMEM((L, B_QK), BF16), pltpu.VMEM((L, B_V), BF16),
               pltpu.VMEM((L, A_KV), BF16), pltpu.VMEM((L, A_KV), BF16)]
    if n_cache:
        for cch in caches:
            in_specs.append(pl.BlockSpec((None,) + cch.shape[1:], lambda b, i: (b, 0, 0)))
            scratch.append(pltpu.VMEM(cch.shape[1:], BF16))
        args += list(caches)
    out_specs = [pl.BlockSpec((tq, ATTN_MIX), lambda b, i: (b * nq + i, 0))]
    out_shape = [jax.ShapeDtypeStruct((nb * L, ATTN_MIX), BF16)]
    if emit_kv:
        for width in (A_KV, A_KV, B_QK, B_V):
            out_specs.append(pl.BlockSpec((L, width), lambda b, i: (b, 0)))
            out_shape.append(jax.ShapeDtypeStruct((nb * L, width), F32))
    outs = pl.pallas_call(
        functools.partial(_attn_kernel, L=L, tq=tq, n_cache=n_cache, rope=rope,
                          window=rope, lam_init=lam_init, emit_kv=emit_kv),
        grid=(nb, nq),
        in_specs=in_specs,
        out_specs=out_specs,
        out_shape=out_shape,
        scratch_shapes=scratch,
        compiler_params=_cparams("parallel", "arbitrary"),
        name=name,
    )(*args)
    return outs if emit_kv else outs[0]


def _rope_tables(L):
    rows = L // GRID_W
    row = jnp.repeat(jnp.arange(rows), GRID_W).astype(F32)
    col = jnp.tile(jnp.arange(GRID_W), rows).astype(F32)
    half = HD // 2
    inv = ROPE_BASE ** (-jnp.arange(0, half, 2, dtype=F32) / half)
    ar = row[:, None] * inv
    ac = col[:, None] * inv
    ang = jnp.concatenate([ar, ar, ac, ac], axis=-1)
    cos, sin = jnp.cos(ang), jnp.sin(ang)
    first = (jnp.arange(HD) % (HD // 2)) < (HD // 4)
    return cos, jnp.where(first, -sin, 0.0), jnp.where(first, 0.0, sin)


CONV_ROWS = 256
CONV_HALO = SUBLANES


def _conv_kernel(x_ref, prev_ref, next_ref, w_ref, b_ref, o_ref, pad, *, n_ctx, lp, ld):
    r0 = pl.program_id(0) * CONV_ROWS
    in_seq = jnp.where(r0 < n_ctx, r0 % lp, (r0 - n_ctx) % ld)
    seq_len = jnp.where(r0 < n_ctx, lp, ld)
    has_prev = in_seq > 0
    has_next = in_seq + CONV_ROWS < seq_len
    zeros = jnp.zeros((CONV_HALO, x_ref.shape[1]), F32)
    pad[0:CONV_HALO, :] = jnp.where(has_prev, prev_ref[...], zeros)
    pad[CONV_HALO:CONV_HALO + CONV_ROWS, :] = x_ref[...]
    pad[CONV_HALO + CONV_ROWS:, :] = jnp.where(has_next, next_ref[...], zeros)
    acc = b_ref[...] + x_ref[...] * w_ref[CONV_K // 2:CONV_K // 2 + 1, :]
    for k in range(CONV_K):
        d = k - CONV_K // 2
        if d != 0:
            acc = acc + pad[CONV_HALO + d:CONV_HALO + d + CONV_ROWS, :] * w_ref[k:k + 1, :]
    o_ref[...] = acc * _sigmoid(acc)


def _conv_silu(zx, conv_w, conv_b, n_ctx, lp, ld):
    n = zx.shape[0]
    tc = 1024
    c0 = D_INNER // tc
    hb = CONV_ROWS // CONV_HALO
    nblk = n // CONV_HALO
    return pl.pallas_call(
        functools.partial(_conv_kernel, n_ctx=n_ctx, lp=lp, ld=ld),
        grid=(n // CONV_ROWS, CONV_CH // tc),
        in_specs=[
            pl.BlockSpec((CONV_ROWS, tc), lambda i, j: (i, c0 + j)),
            pl.BlockSpec((CONV_HALO, tc), lambda i, j: (jnp.maximum(i * hb - 1, 0), c0 + j)),
            pl.BlockSpec((CONV_HALO, tc), lambda i, j: (jnp.minimum((i + 1) * hb, nblk - 1), c0 + j)),
            pl.BlockSpec((CONV_K, tc), lambda i, j: (0, j)),
            pl.BlockSpec((1, tc), lambda i, j: (0, j)),
        ],
        out_specs=pl.BlockSpec((CONV_ROWS, tc), lambda i, j: (i, j)),
        out_shape=jax.ShapeDtypeStruct((n, CONV_CH), F32),
        scratch_shapes=[pltpu.VMEM((CONV_ROWS + 2 * CONV_HALO, tc), F32)],
        compiler_params=_cparams("parallel", "parallel"),
        name="ssd_conv",
    )(zx, zx, zx, conv_w, conv_b.reshape(1, CONV_CH))


def _chunk_info(gc, nseq_c, ncc, ncl):
    n_c = nseq_c * ncc
    is_l = gc >= n_c
    seq = jnp.where(is_l, nseq_c + (gc - n_c) // ncl, gc // ncc)
    c = jnp.where(is_l, (gc - n_c) % ncl, gc % ncc)
    nc = jnp.where(is_l, ncl, ncc)
    return seq, c, nc, gc - c, is_l


def _split3(v):
    hi = v.astype(BF16)
    r1 = v - hi.astype(F32)
    mid = r1.astype(BF16)
    lo = (r1 - mid.astype(F32)).astype(BF16)
    return jnp.concatenate([hi, mid, lo], axis=1)


def _ssd_kernel(x_ref, b_ref, c_ref, dt_ref, dtb_ref, alog_ref, h0_ref, e3_ref, y_ref, hfin_ref, h_scr,
                *, nseq_c, ncc, ncl):
    fwd = pl.program_id(0) == 0
    _, c, nc, _, is_l = _chunk_info(pl.program_id(1), nseq_c, ncc, ncl)

    @pl.when(c == 0)
    def _():
        h_scr[...] = jnp.where(is_l, h0_ref[...], jnp.zeros_like(h_scr))

    r = lax.broadcasted_iota(jnp.int32, (CHUNK, CHUNK), 0)
    s = lax.broadcasted_iota(jnp.int32, (CHUNK, CHUNK), 1)
    incl = jnp.where(fwd, r - s, s - r) >= 0
    raw = dt_ref[...] + dtb_ref[...]
    dt = jnp.maximum(raw, 0.0) + jnp.log(1.0 + jnp.exp(-jnp.abs(raw)))
    dta = dt * (-jnp.exp(alog_ref[...]))
    ac = _dot(jnp.where(incl, 1.0, 0.0), dta, precision=HIGHEST)
    ac_last = jnp.where(fwd, ac[CHUNK - 1:CHUNK, :], ac[0:1, :])
    eac = jnp.exp(ac)
    dtd = dt * jnp.exp(ac_last - ac)
    cdec = jnp.exp(ac_last)
    rows_t = jnp.transpose(jnp.concatenate([ac, dt], axis=1))
    lane = lax.broadcasted_iota(jnp.int32, (CHUNK, 2 * SSD_HEADDIM), 1)
    first = lane < SSD_HEADDIM
    pw = 2 * SSD_HEADDIM
    gw = SSD_HPG * SSD_HEADDIM
    dtd3 = _split3(dtd)
    eac3 = _split3(eac)

    for g in range(SSD_GROUPS):
        cols = slice(g * gw, (g + 1) * gw)
        dtd_e = _dot(dtd3, e3_ref[:, cols])
        eac_e = _dot(eac3, e3_ref[:, cols])
        bg = b_ref[:, g * SSD_STATE:(g + 1) * SSD_STATE].astype(BF16)
        cg = c_ref[:, g * SSD_STATE:(g + 1) * SSD_STATE].astype(BF16)
        scores = _dot_nt(cg, bg)
        xg = x_ref[:, cols]
        h_in = h_scr[cols, :]
        y_off = _dot_nt(cg, h_in.astype(BF16)) * eac_e
        st = _dot_tn((xg * dtd_e).astype(BF16), bg)
        yd = []
        for pp in range(SSD_HPG // 2):
            h0 = g * SSD_HPG + 2 * pp
            xp = xg[:, pp * pw:(pp + 1) * pw].astype(BF16)
            two = []
            for h in (h0, h0 + 1):
                decay = jnp.exp(jnp.where(incl, ac[:, h:h + 1] - rows_t[h:h + 1, :], NEG))
                mm = (scores * decay * rows_t[SSD_HEADS + h:SSD_HEADS + h + 1, :]).astype(BF16)
                two.append(_dot(mm, xp))
            yd.append(jnp.where(first, two[0], two[1]))
        y_ref[:, cols] = (jnp.concatenate(yd, axis=1) + y_off).astype(y_ref.dtype)
        for hh in range(SSD_HPG):
            h = g * SSD_HPG + hh
            lo = hh * SSD_HEADDIM
            h_scr[h * SSD_HEADDIM:(h + 1) * SSD_HEADDIM, :] = (
                cdec[:, h:h + 1] * h_in[lo:lo + SSD_HEADDIM, :] + st[lo:lo + SSD_HEADDIM, :])

    @pl.when(jnp.logical_and(c == nc - 1, jnp.logical_not(is_l)))
    def _():
        hfin_ref[...] = h_scr[...]


def _ssd_scan(xc, dt2, dt_bias, a_log, h0, nseq_c, lp, ld):
    n = xc.shape[0]
    ncc, ncl = lp // CHUNK, ld // CHUNK
    nchunks = n // CHUNK
    bd = h0.shape[1]
    sw = SSD_GROUPS * SSD_STATE
    info = functools.partial(_chunk_info, nseq_c=nseq_c, ncc=ncc, ncl=ncl)

    def rowblk(d, gc):
        _, c, nc, start, _ = info(gc)
        return start + jnp.where(d == 0, c, nc - 1 - c)

    def seq_l(gc):
        return jnp.clip(info(gc)[0] - nseq_c, 0, bd - 1)

    def seq_c(gc):
        return jnp.minimum(info(gc)[0], nseq_c - 1)

    head_of_col = jnp.arange(D_INNER) // SSD_HEADDIM
    expand3 = (jnp.arange(3 * SSD_HEADS)[:, None] % SSD_HEADS == head_of_col[None, :]).astype(BF16)

    return pl.pallas_call(
        functools.partial(_ssd_kernel, nseq_c=nseq_c, ncc=ncc, ncl=ncl),
        grid=(2, nchunks),
        in_specs=[
            pl.BlockSpec((CHUNK, D_INNER), lambda d, gc: (rowblk(d, gc), 0)),
            pl.BlockSpec((CHUNK, sw), lambda d, gc: (rowblk(d, gc), D_INNER // sw)),
            pl.BlockSpec((CHUNK, sw), lambda d, gc: (rowblk(d, gc), D_INNER // sw + 1)),
            pl.BlockSpec((None, CHUNK, SSD_HEADS), lambda d, gc: (d, rowblk(d, gc), 0)),
            pl.BlockSpec((None, 1, SSD_HEADS), lambda d, gc: (d, 0, 0)),
            pl.BlockSpec((None, 1, SSD_HEADS), lambda d, gc: (d, 0, 0)),
            pl.BlockSpec((None, None, D_INNER, SSD_STATE), lambda d, gc: (d, seq_l(gc), 0, 0)),
            pl.BlockSpec((3 * SSD_HEADS, D_INNER), lambda d, gc: (0, 0)),
        ],
        out_specs=[
            pl.BlockSpec((None, CHUNK, D_INNER), lambda d, gc: (d, rowblk(d, gc), 0)),
            pl.BlockSpec((None, None, D_INNER, SSD_STATE), lambda d, gc: (d, seq_c(gc), 0, 0)),
        ],
        out_shape=[
            jax.ShapeDtypeStruct((2, n, D_INNER), BF16),
            jax.ShapeDtypeStruct((2, nseq_c, D_INNER, SSD_STATE), F32),
        ],
        scratch_shapes=[pltpu.VMEM((D_INNER, SSD_STATE), F32)],
        compiler_params=_cparams("arbitrary", "arbitrary"),
        name="ssd_scan",
    )(xc, xc, xc, dt2, dt_bias.reshape(2, 1, SSD_HEADS), a_log.reshape(2, 1, SSD_HEADS), h0, expand3)


def _ssd_gate_kernel(yf_ref, yb_ref, xs_ref, z_ref, d_ref, g_ref, o_ref):
    z = z_ref[...]
    y = (yf_ref[...].astype(F32) + yb_ref[...].astype(F32) + d_ref[...] * xs_ref[...]) * (z * _sigmoid(z))
    gs = D_INNER // SSD_GROUPS
    for g in range(SSD_GROUPS):
        yg = y[:, g * gs:(g + 1) * gs]
        n = yg * lax.rsqrt(jnp.mean(yg * yg, axis=-1, keepdims=True) + EPS)
        o_ref[:, g * gs:(g + 1) * gs] = (n * g_ref[:, g * gs:(g + 1) * gs]).astype(BF16)


def _ssd_gate(y2, xc, zx, d_skip, norm_g):
    n = xc.shape[0]
    tm = 128
    d_e = jnp.repeat(d_skip, SSD_HEADDIM).reshape(1, D_INNER)
    return pl.pallas_call(
        _ssd_gate_kernel,
        grid=(n // tm,),
        in_specs=[
            pl.BlockSpec((None, tm, D_INNER), lambda i: (0, i, 0)),
            pl.BlockSpec((None, tm, D_INNER), lambda i: (1, i, 0)),
            pl.BlockSpec((tm, D_INNER), lambda i: (i, 0)),
            pl.BlockSpec((tm, D_INNER), lambda i: (i, 0)),
            pl.BlockSpec((1, D_INNER), lambda i: (0, 0)),
            pl.BlockSpec((1, D_INNER), lambda i: (0, 0)),
        ],
        out_specs=pl.BlockSpec((tm, D_INNER), lambda i: (i, 0)),
        out_shape=jax.ShapeDtypeStruct((n, D_INNER), BF16),
        compiler_params=_cparams("parallel"),
        name="ssd_gate",
    )(y2, y2, xc, zx, d_e, norm_g.reshape(1, D_INNER))


def _oddeven_merge(lo, hi, r):
    step = r * 2
    if step < hi - lo:
        yield from _oddeven_merge(lo, hi, step)
        yield from _oddeven_merge(lo + r, hi, step)
        for i in range(lo + r, hi - r, step):
            yield (i, i + r)
    else:
        yield (lo, lo + r)


def _oddeven_sort(lo, hi):
    if hi - lo >= 1:
        mid = lo + (hi - lo) // 2
        yield from _oddeven_sort(lo, mid)
        yield from _oddeven_sort(mid + 1, hi)
        yield from _oddeven_merge(lo, hi, 1)


def _cmpx(v, i, j):
    a, b = v[i], v[j]
    if b is None:
        return
    if a is None:
        v[i], v[j] = b, None
        return
    v[i], v[j] = jnp.maximum(a, b), jnp.minimum(a, b)


def _top_across_sublanes(v):
    n = len(v)
    for i, j in _oddeven_sort(0, n - 1):
        _cmpx(v, i, j)
    for shift in (4, 2, 1):
        rolled = [None if a is None else pltpu.roll(a, shift, 0) for a in v]
        c = []
        for i in range(n):
            a, b = v[i], rolled[n - 1 - i]
            c.append(a if b is None else (b if a is None else jnp.maximum(a, b)))
        stride = n // 2
        while stride >= 1:
            for i in range(n):
                if i & stride == 0:
                    _cmpx(c, i, i + stride)
            stride //= 2
        v = c
    return v


def _topk_kernel(q_ref, sk_ref, s1_ref, s2_ref, aux_ref, s_scr, sv_scr, aux_scr):
    tk = q_ref.shape[0]
    ng = tk // LANES
    nhc = 2 * PEER_HEADS
    for hc in range(nhc):
        qh = q_ref[:, hc * PEER_DHALF:(hc + 1) * PEER_DHALF].astype(BF16)
        s_t = _dot_nt(sk_ref[hc].astype(BF16), qh)
        if hc % 2 == 0:
            s1_ref[hc // 2] = s_t
        else:
            s2_ref[hc // 2] = s_t
        for g in range(ng):
            s_scr[g * nhc + hc] = s_t[:, g * LANES:(g + 1) * LANES]

    def stage1(k, carry):
        x = s_scr[k]
        v = _top_across_sublanes([x[SUBLANES * i:SUBLANES * (i + 1), :] for i in range(PEER_TOPK)])
        sv_scr[k] = jnp.concatenate([a[0:1, :] for a in v], axis=0)
        return carry

    lax.fori_loop(0, ng * nhc, stage1, 0)

    def stage2(k, carry):
        g = k // PEER_HEADS
        h = k % PEER_HEADS
        sv1 = sv_scr[g * nhc + 2 * h]
        sv2 = sv_scr[g * nhc + 2 * h + 1]
        b1 = [jnp.broadcast_to(sv1[i:i + 1, :], (SUBLANES, LANES)) for i in range(SUBLANES)]
        b20 = jnp.broadcast_to(sv2[0:1, :], (SUBLANES, LANES))
        lo2, hi2, hi1 = sv2[0:SUBLANES, :], sv2[SUBLANES:, :], sv1[SUBLANES:, :]
        cands = [b1[0] + lo2, b1[0] + hi2] + [b1[i] + lo2 for i in range(1, SUBLANES)] + [hi1 + b20]
        t = _top_across_sublanes(cands + [None] * (PEER_TOPK - len(cands)))
        z = jnp.zeros_like(t[0])
        for a in t:
            z = z + jnp.exp(a - t[0])
        aux_scr[k] = jnp.concatenate(
            [t[PEER_TOPK - 1][0:1, :],
             b1[0][0:1, :],
             b20[0:1, :],
             z[0:1, :]], axis=0)
        return carry

    lax.fori_loop(0, ng * PEER_HEADS, stage2, 0)
    for g in range(ng):
        for h in range(PEER_HEADS):
            aux_ref[h, :, g * LANES:(g + 1) * LANES] = aux_scr[g * PEER_HEADS + h]


def _peer_topk(q, sub_keys):
    n = q.shape[0]
    tk = 2 * LANES
    ng = tk // LANES
    sk = sub_keys.reshape(2 * PEER_HEADS, PEER_KEYS, PEER_DHALF)
    return pl.pallas_call(
        _topk_kernel,
        grid=(n // tk,),
        in_specs=[
            pl.BlockSpec((tk, PEER_HEADS * PEER_DKEY), lambda i: (i, 0)),
            pl.BlockSpec(sk.shape, lambda i: (0, 0, 0)),
        ],
        out_specs=[
            pl.BlockSpec((PEER_HEADS, PEER_KEYS, tk), lambda i: (0, 0, i)),
            pl.BlockSpec((PEER_HEADS, PEER_KEYS, tk), lambda i: (0, 0, i)),
            pl.BlockSpec((PEER_HEADS, 4, tk), lambda i: (0, 0, i)),
        ],
        out_shape=[
            jax.ShapeDtypeStruct((PEER_HEADS, PEER_KEYS, n), F32),
            jax.ShapeDtypeStruct((PEER_HEADS, PEER_KEYS, n), F32),
            jax.ShapeDtypeStruct((PEER_HEADS, 4, n), F32),
        ],
        scratch_shapes=[
            pltpu.VMEM((ng * 2 * PEER_HEADS, PEER_KEYS, LANES), F32),
            pltpu.VMEM((ng * 2 * PEER_HEADS, PEER_TOPK, LANES), F32),
            pltpu.VMEM((ng * PEER_HEADS, 4, LANES), F32),
        ],
        compiler_params=_cparams("parallel"),
        name="peer_topk",
    )(q, sk)


_GELU_K0 = -2.0 * math.sqrt(2.0 / math.pi) * math.log2(math.e)
_GELU_K1 = 0.044715 * _GELU_K0


def _gelu_tanh(x):
    return x / (1.0 + jnp.exp2(x * (_GELU_K0 + _GELU_K1 * (x * x))))


PEER_ROWS = 64


def _peer_kernel(hb_ref, u_ref, v_ref, s1_ref, s2_ref, aux_ref, x_ref, gate_ref, o_ref,
                 acc, st, act, a2, m1z):
    e = pl.program_id(1)
    te, tm = st.shape

    @pl.when(e == 0)
    def _():
        acc[...] = jnp.zeros_like(acc)
        for h in range(PEER_HEADS):
            a2[h] = jnp.exp(s2_ref[h] - aux_ref[h, 2:3, :])
            m1z[h:h + 1, :] = aux_ref[h, 1:2, :] + jnp.log(aux_ref[h, 3:4, :])

    st[...] = _dot_nt(u_ref[...], hb_ref[...])
    for a in range(te // PEER_KEYS):
        for tc in range(tm // LANES):
            ln = slice(tc * LANES, (tc + 1) * LANES)
            s1row = [s1_ref[h, a:a + 1, ln] for h in range(PEER_HEADS)]
            a1row = [jnp.exp(s1row[h] - m1z[h:h + 1, ln]) for h in range(PEER_HEADS)]
            for k0 in range(0, PEER_KEYS, PEER_ROWS):
                kr = slice(k0, k0 + PEER_ROWS)
                w = None
                for h in range(PEER_HEADS):
                    sel = (s1row[h] + s2_ref[h, kr, ln]) >= aux_ref[h, 0:1, ln]
                    t = jnp.where(sel, a2[h, kr, ln], 0.0) * a1row[h]
                    w = t if w is None else w + t
                rows = slice(a * PEER_KEYS + k0, a * PEER_KEYS + k0 + PEER_ROWS)
                act[rows, ln] = (_gelu_tanh(st[rows, ln]) * w).astype(BF16)
    acc[...] += _dot_tn(act[...], v_ref[...])

    @pl.when(e == pl.num_programs(1) - 1)
    def _():
        o_ref[...] = x_ref[...] + gate_ref[0] * acc[...]


def _peer_dense(hb, u_b, v_b, s1, s2, aux, x, mod, gate_k, n_ctx, ld, tm=512, te=1024):
    n, d = x.shape
    ne = PEER_EXPERTS // te
    cond = functools.partial(_cond_of_tile, tm=tm, n_ctx=n_ctx, ld=ld)
    once = dict(pipeline_mode=pl.Buffered(1))
    return pl.pallas_call(
        _peer_kernel,
        grid=(n // tm, ne),
        in_specs=[
            pl.BlockSpec((tm, d), lambda i, e: (i, 0), **once),
            pl.BlockSpec((te, d), lambda i, e: (e, 0)),
            pl.BlockSpec((te, d), lambda i, e: (e, 0)),
            pl.BlockSpec((PEER_HEADS, te // PEER_KEYS, tm), lambda i, e: (0, e, i)),
            pl.BlockSpec((PEER_HEADS, PEER_KEYS, tm), lambda i, e: (0, 0, i), **once),
            pl.BlockSpec((PEER_HEADS, 4, tm), lambda i, e: (0, 0, i)),
            pl.BlockSpec((tm, d), lambda i, e: (i, 0), **once),
            pl.BlockSpec((1, 1, d), lambda i, e: (cond(i) * 6 + gate_k, 0, 0)),
        ],
        out_specs=pl.BlockSpec((tm, d), lambda i, e: (i, 0)),
        out_shape=jax.ShapeDtypeStruct((n, d), F32),
        scratch_shapes=[
            pltpu.VMEM((tm, d), F32),
            pltpu.VMEM((te, tm), F32),
            pltpu.VMEM((te, tm), BF16),
            pltpu.VMEM((PEER_HEADS, PEER_KEYS, tm), F32),
            pltpu.VMEM((PEER_HEADS, tm), F32),
        ],
        compiler_params=_cparams("parallel", "arbitrary"),
        name="peer_dense",
    )(hb, u_b, v_b, s1, s2, aux, x, mod)


def _peer_layer(x, mod, norm_g, w_q, sub_keys, u, v, n_ctx, ld, tm):
    q, hb = _proj(x, w_q.astype(BF16), tm=tm, tn=w_q.shape[1], n_ctx=n_ctx, ld=ld, name="peer_q", mod=mod,
                  norm_g=norm_g, shift_k=3, scale_k=4, emit_h=True)
    s1, s2, aux = _peer_topk(q, sub_keys)
    return _peer_dense(hb, u.astype(BF16), v.astype(BF16), s1, s2, aux, x, mod, 5, n_ctx, ld)


def _rmsnorm_kernel(x_ref, g_ref, oc_ref, ol_ref, *, ctx_tiles):
    x = x_ref[...]
    y = x * lax.rsqrt(jnp.mean(x * x, axis=-1, keepdims=True) + EPS) * g_ref[...]
    i = pl.program_id(0)

    @pl.when(i < ctx_tiles)
    def _():
        oc_ref[...] = y

    @pl.when(i >= ctx_tiles)
    def _():
        ol_ref[...] = y


def _final_norm(x, g, n_ctx):
    n, d = x.shape
    tm = 512
    ctx_tiles = n_ctx // tm
    return pl.pallas_call(
        functools.partial(_rmsnorm_kernel, ctx_tiles=ctx_tiles),
        grid=(n // tm,),
        in_specs=[pl.BlockSpec((tm, d), lambda i: (i, 0)), pl.BlockSpec((1, d), lambda i: (0, 0))],
        out_specs=[pl.BlockSpec((tm, d), lambda i: (jnp.minimum(i, ctx_tiles - 1), 0)),
                   pl.BlockSpec((tm, d), lambda i: (jnp.maximum(i - ctx_tiles, 0), 0))],
        out_shape=[jax.ShapeDtypeStruct((n_ctx, d), F32), jax.ShapeDtypeStruct((n - n_ctx, d), F32)],
        compiler_params=_cparams("arbitrary"),
        name="final_norm",
    )(x, g.reshape(1, d))


def _diff_lambda_init(layer):
    return 0.8 - 0.6 * math.exp(-0.3 * layer)


def kernel(x_prompt, x_sample, cache_a_k, cache_a_v, cache_b_k, cache_b_v, state_ssd_fwd, state_ssd_bwd, c, c_ctx, w_mod, b_mod, norm_g, attn_w_in, attn_w_out, attn_sink, diff_lam, diff_subln_g, ssd_w_in, ssd_conv_w, ssd_conv_b, ssd_dt_bias, ssd_a_log, ssd_d, ssd_norm_g, ssd_w_out, peer_w_q, peer_sub_keys, peer_u, peer_v, final_g):
    bp, lp, d = x_prompt.shape
    bd, ld, _ = x_sample.shape
    past = cache_a_k.shape[2]
    n_ctx, n_lat = bp * lp, bd * ld
    tm = 512
    assert n_ctx % ld == 0 and n_ctx % tm == 0 and ld % tm == 0
    assert lp % CONV_ROWS == 0 and ld % CONV_ROWS == 0

    x = jnp.concatenate([x_prompt.reshape(n_ctx, d), x_sample.reshape(n_lat, d)], axis=0)

    r = -(-(bd + 1) // SUBLANES) * SUBLANES
    cond = jnp.concatenate([c_ctx[None], c, jnp.zeros((r - bd - 1, d), F32)], axis=0)
    mod_all = _modulation(cond, w_mod, b_mod).reshape(DEPTH, r * 6, 1, d)

    outs = {}
    for l in range(DEPTH):
        i = l // 2
        mod = mod_all[l]
        if l % 2 == 0:
            lam_init = _diff_lambda_init(l)
            w_in = attn_w_in[i]
            sp = (0, A_Q, A_Q + A_KV, A_Q + 2 * A_KV, A_Q + 2 * A_KV + B_QK, A_Q + 2 * A_KV + 2 * B_QK, ATTN_IN)
            aq, ak, av, bq, bk, bv = (w_in[:, sp[k]:sp[k + 1]] for k in range(6))
            w_perm = jnp.concatenate([aq, bq, bk, bv, ak, av], axis=1).astype(BF16)
            p = _proj(x, w_perm, tm=tm, tn=1152, n_ctx=n_ctx, ld=ld, name="attn_in", mod=mod,
                      norm_g=norm_g[l, 0], shift_k=0, scale_k=1)
            mix_c, ak_c, av_c, bk_c, bv_c = _attention(
                p, 0, bp, lp, lp, attn_sink[i], diff_lam[i], diff_subln_g[i], lam_init, "attn_ctx",
                emit_kv=True)
            caches = (cache_a_k[:, i].reshape(bd, past, A_KV), cache_a_v[:, i].reshape(bd, past, A_KV),
                      cache_b_k[:, i].reshape(bd, past, B_QK), cache_b_v[:, i].reshape(bd, past, B_V))
            mix_l = _attention(p, n_ctx, bd, ld, WINDOW, attn_sink[i], diff_lam[i], diff_subln_g[i],
                               lam_init, "attn_lat", caches=caches, rope_tabs=_rope_tables(ld))
            x = _proj((mix_c, mix_l), attn_w_out[i].astype(BF16), tm=tm, tn=d, n_ctx=n_ctx, ld=ld,
                      name="attn_out", mod=mod, res=x, gate_k=2)
            outs.setdefault("ak", []).append(ak_c.reshape(bp, lp, A_KV_HEADS, HD))
            outs.setdefault("av", []).append(av_c.reshape(bp, lp, A_KV_HEADS, HD))
            outs.setdefault("bk", []).append(bk_c.reshape(bp, lp, B_HEADS, 2, HD))
            outs.setdefault("bv", []).append(bv_c.reshape(bp, lp, B_HEADS, B_VD))
        else:
            w_in = ssd_w_in[i]
            nzx = D_INNER + CONV_CH
            zx = _proj(x, w_in[:, :nzx].astype(BF16), tm=tm, tn=2048, n_ctx=n_ctx, ld=ld, name="ssd_in",
                       mod=mod, norm_g=norm_g[l, 0], shift_k=0, scale_k=1)
            dtr = _proj(x, w_in[:, nzx:].astype(BF16), tm=tm, tn=2 * SSD_HEADS, n_ctx=n_ctx, ld=ld,
                        name="ssd_dt", mod=mod, norm_g=norm_g[l, 0], shift_k=0, scale_k=1)
            dt2 = jnp.transpose(dtr.reshape(-1, 2, SSD_HEADS), (1, 0, 2))
            xc = _conv_silu(zx, ssd_conv_w[i], ssd_conv_b[i], n_ctx, lp, ld)
            h0 = jnp.stack([state_ssd_fwd[:, i], state_ssd_bwd[:, i]]).reshape(2, bd, D_INNER, SSD_STATE)
            y2, hfin = _ssd_scan(xc, dt2, ssd_dt_bias[i], ssd_a_log[i], h0, bp, lp, ld)
            hg = _ssd_gate(y2, xc, zx, ssd_d[i], ssd_norm_g[i])
            x = _proj(hg, ssd_w_out[i].astype(BF16), tm=tm, tn=d, n_ctx=n_ctx, ld=ld, name="ssd_out",
                      mod=mod, res=x, gate_k=2)
            hs = hfin.reshape(2, bp, SSD_HEADS, SSD_HEADDIM, SSD_STATE)
            outs.setdefault("sf", []).append(hs[0])
            outs.setdefault("sb", []).append(hs[1])
        x = _peer_layer(x, mod, norm_g[l, 1], peer_w_q[l], peer_sub_keys[l], peer_u[l], peer_v[l],
                        n_ctx, ld, tm)

    y_c, y_l = _final_norm(x, final_g, n_ctx)
    y_prompt = y_c.reshape(bp, lp, d)
    y_sample = y_l.reshape(bd, ld, d)
    return (y_prompt, y_sample, jnp.stack(outs["ak"], axis=1), jnp.stack(outs["av"], axis=1),
            jnp.stack(outs["bk"], axis=1), jnp.stack(outs["bv"], axis=1),
            jnp.stack(outs["sf"], axis=1), jnp.stack(outs["sb"], axis=1))
```

```python
import functools
import math

import jax
import jax.numpy as jnp
from jax import lax
from jax.experimental import pallas as pl
from jax.experimental.pallas import tpu as pltpu

F32 = jnp.float32
BF16 = jnp.bfloat16

D_MODEL = 2048
DEPTH = 2
GRID_W = 64
EPS = 1e-6
NEG = -1e30

HD = 128
A_HEADS = 8
A_KV_HEADS = 2
A_GROUP = A_HEADS // A_KV_HEADS
WINDOW = 128
B_HEADS = 4
B_VD = 2 * HD
A_Q = A_HEADS * HD
A_KV = A_KV_HEADS * HD
B_QK = B_HEADS * 2 * HD
B_V = B_HEADS * B_VD
ATTN_IN = A_Q + 2 * A_KV + 2 * B_QK + B_V
ATTN_MIX = A_Q + B_V
ROPE_BASE = 10000.0

D_INNER = 2 * D_MODEL
SSD_HEADDIM = 64
SSD_HEADS = D_INNER // SSD_HEADDIM
SSD_GROUPS = 8
SSD_HPG = SSD_HEADS // SSD_GROUPS
SSD_STATE = 128
CONV_K = 5
CONV_CH = D_INNER + 2 * SSD_GROUPS * SSD_STATE
CHUNK = 128

PEER_HEADS = 8
PEER_KEYS = 128
PEER_EXPERTS = PEER_KEYS * PEER_KEYS
PEER_TOPK = 16
PEER_DKEY = 256
PEER_DHALF = PEER_DKEY // 2

V7X_VMEM_LIMIT_BYTES = 56 * 1024 * 1024
LANES = 128
SUBLANES = 8

HIGHEST = lax.Precision.HIGHEST


def _cparams(*sem):
    return pltpu.CompilerParams(dimension_semantics=sem, vmem_limit_bytes=V7X_VMEM_LIMIT_BYTES)


def _dot(a, b, precision=None):
    return jnp.dot(a, b, preferred_element_type=F32, precision=precision)


def _dot_nt(a, b):
    return lax.dot_general(a, b, (((1,), (1,)), ((), ())), preferred_element_type=F32)


def _dot_tn(a, b):
    return lax.dot_general(a, b, (((0,), (0,)), ((), ())), preferred_element_type=F32)


def _sigmoid(x):
    return 1.0 / (1.0 + jnp.exp(-x))


def _cond_of_tile(i, tm, n_ctx, ld):
    r = i * tm
    return jnp.where(r < n_ctx, 0, 1 + (r - n_ctx) // ld)


def _mod_kernel(c_ref, w_ref, b_ref, o_ref):
    c = c_ref[...]
    s = c * _sigmoid(c)
    o_ref[...] = _dot(s.astype(BF16), w_ref[...].astype(BF16)) + b_ref[...]


def _modulation(cond, w_mod, b_mod):
    depth, d, n6 = w_mod.shape
    r = cond.shape[0]
    tn = 1024
    return pl.pallas_call(
        _mod_kernel,
        grid=(depth, n6 // tn),
        in_specs=[
            pl.BlockSpec((r, d), lambda l, j: (0, 0)),
            pl.BlockSpec((None, d, tn), lambda l, j: (l, 0, j)),
            pl.BlockSpec((None, 1, tn), lambda l, j: (l, 0, j)),
        ],
        out_specs=pl.BlockSpec((None, r, tn), lambda l, j: (l, 0, j)),
        out_shape=jax.ShapeDtypeStruct((depth, r, n6), F32),
        compiler_params=_cparams("parallel", "parallel"),
        name="modulation",
    )(cond, w_mod, b_mod.reshape(depth, 1, n6))


def _proj_kernel(*refs, adaln, emit_h, residual, split_tiles, res_split_tiles):
    it = iter(refs)
    x_ref = next(it)
    if split_tiles:
        x2_ref = next(it)
    if adaln:
        g_ref, sh_ref, sc_ref = next(it), next(it), next(it)
    w_ref = next(it)
    if residual:
        r_ref = next(it)
        if res_split_tiles:
            r2_ref = next(it)
        gate_ref = next(it)
    o_ref = next(it)
    if emit_h:
        h_ref = next(it)

    def rows(a_ref, b_ref, tiles):
        return jnp.where(pl.program_id(0) < tiles, a_ref[...], b_ref[...])

    if adaln:
        hs_ref = next(it)
        j = pl.program_id(1)

        @pl.when(j == 0)
        def _():
            x = rows(x_ref, x2_ref, split_tiles) if split_tiles else x_ref[...]
            y = x * lax.rsqrt(jnp.mean(x * x, axis=-1, keepdims=True) + EPS)
            y = y * g_ref[...]
            y = y * (1.0 + sc_ref[0]) + sh_ref[0]
            hb = y.astype(BF16)
            hs_ref[...] = hb
            if emit_h:
                h_ref[...] = hb

        h = hs_ref[...]
    elif split_tiles:
        h = rows(x_ref, x2_ref, split_tiles)
    else:
        h = x_ref[...]
    acc = _dot(h, w_ref[...])
    if residual:
        r = rows(r_ref, r2_ref, res_split_tiles) if res_split_tiles else r_ref[...]
        o_ref[...] = r + gate_ref[0] * acc
    else:
        o_ref[...] = acc


def _proj(x, w, *, tm, tn, n_ctx, ld, name, mod=None, norm_g=None, shift_k=None, scale_k=None,
          emit_h=False, res=None, gate_k=None):
    split = isinstance(x, tuple)
    if split:
        x, x2 = x
        split_tiles = x.shape[0] // tm
        n, k = x.shape[0] + x2.shape[0], x.shape[1]
    else:
        split_tiles = 0
        n, k = x.shape
    m = w.shape[1]
    adaln = norm_g is not None
    residual = res is not None
    cond = functools.partial(_cond_of_tile, tm=tm, n_ctx=n_ctx, ld=ld)
    if split:
        in_specs = [pl.BlockSpec((tm, k), lambda i, j: (jnp.minimum(i, split_tiles - 1), 0)),
                    pl.BlockSpec((tm, k), lambda i, j: (jnp.maximum(i - split_tiles, 0), 0))]
        args = [x, x2]
    else:
        in_specs = [pl.BlockSpec((tm, k), lambda i, j: (i, 0))]
        args = [x]
    if adaln:
        in_specs += [
            pl.BlockSpec((1, k), lambda i, j: (0, 0)),
            pl.BlockSpec((1, 1, k), lambda i, j: (cond(i) * 6 + shift_k, 0, 0)),
            pl.BlockSpec((1, 1, k), lambda i, j: (cond(i) * 6 + scale_k, 0, 0)),
        ]
        args += [norm_g.reshape(1, k), mod, mod]
    w_mode = dict(pipeline_mode=pl.Buffered(1)) if tn == m else {}
    in_specs.append(pl.BlockSpec((k, tn), lambda i, j: (0, j), **w_mode))
    args.append(w)
    res_split_tiles = 0
    if residual:
        if isinstance(res, tuple):
            res_split_tiles = res[0].shape[0] // tm
            in_specs += [
                pl.BlockSpec((tm, tn), lambda i, j: (jnp.minimum(i, res_split_tiles - 1), j)),
                pl.BlockSpec((tm, tn), lambda i, j: (jnp.maximum(i - res_split_tiles, 0), j)),
            ]
            args += list(res)
        else:
            in_specs.append(pl.BlockSpec((tm, tn), lambda i, j: (i, j)))
            args.append(res)
        in_specs.append(pl.BlockSpec((1, 1, tn), lambda i, j: (cond(i) * 6 + gate_k, 0, j)))
        args.append(mod)
    out_specs = [pl.BlockSpec((tm, tn), lambda i, j: (i, j))]
    out_shape = [jax.ShapeDtypeStruct((n, m), F32)]
    if emit_h:
        out_specs.append(pl.BlockSpec((tm, k), lambda i, j: (i, 0)))
        out_shape.append(jax.ShapeDtypeStruct((n, k), BF16))
    scratch = [pltpu.VMEM((tm, k), BF16)] if adaln else []
    outs = pl.pallas_call(
        functools.partial(_proj_kernel, adaln=adaln, emit_h=emit_h, residual=residual,
                          split_tiles=split_tiles, res_split_tiles=res_split_tiles),
        grid=(n // tm, m // tn),
        in_specs=in_specs,
        out_specs=out_specs,
        out_shape=out_shape,
        scratch_shapes=scratch,
        compiler_params=_cparams("parallel", "arbitrary"),
        name=name,
    )(*args)
    return outs if emit_h else outs[0]


def _attn_kernel(*refs, L, tq, n_cache, rope, window, lam_init, emit_kv):
    it = iter(refs)
    sink_ref = next(it)
    lam_ref = next(it)
    sub_ref = next(it)
    if rope:
        cos_ref, sa_ref, sb_ref = next(it), next(it), next(it)
    qa_ref, qb_ref, kb_ref, vb_ref, kva_ref = (next(it) for _ in range(5))
    if n_cache:
        cka_ref, cva_ref, ckb_ref, cvb_ref = (next(it) for _ in range(4))
    o_ref = next(it)
    if emit_kv:
        ako_ref, avo_ref, bko_ref, bvo_ref = (next(it) for _ in range(4))
    kb_s, vb_s, ka_s, va_s = (next(it) for _ in range(4))
    if n_cache:
        cka_s, cva_s, ckb_s, cvb_s = (next(it) for _ in range(4))
    i = pl.program_id(1)

    def rope_fn(x, r0, rows):
        if not rope:
            return x
        c = cos_ref[pl.ds(r0, rows), :]
        sa = sa_ref[pl.ds(r0, rows), :]
        sb = sb_ref[pl.ds(r0, rows), :]
        return x * c + pltpu.roll(x, 96, 1) * sa + pltpu.roll(x, 32, 1) * sb

    @pl.when(i == 0)
    def _():
        for g in range(A_KV_HEADS):
            ka_s[:, g * HD:(g + 1) * HD] = rope_fn(kva_ref[:, g * HD:(g + 1) * HD], 0, L).astype(BF16)
        va_s[...] = kva_ref[:, A_KV:2 * A_KV].astype(BF16)
        for c in range(2 * B_HEADS):
            kb_s[:, c * HD:(c + 1) * HD] = rope_fn(kb_ref[:, c * HD:(c + 1) * HD], 0, L).astype(BF16)
        vb_s[...] = vb_ref[...].astype(BF16)
        if emit_kv:
            ako_ref[...] = kva_ref[:, 0:A_KV]
            avo_ref[...] = kva_ref[:, A_KV:2 * A_KV]
            bko_ref[...] = kb_ref[...]
            bvo_ref[...] = vb_ref[...]
        if n_cache:
            cka_s[...] = cka_ref[...].astype(BF16)
            cva_s[...] = cva_ref[...].astype(BF16)
            ckb_s[...] = ckb_ref[...].astype(BF16)
            cvb_s[...] = cvb_ref[...].astype(BF16)

    q0 = pl.multiple_of(i * tq, tq)
    scale = HD ** -0.5

    if window:
        wlen = 3 * WINDOW
        start = pl.multiple_of(jnp.clip((i - 1) * WINDOW, 0, L - wlen), WINDOW)
    else:
        wlen = L
        start = 0
    for g in range(A_KV_HEADS):
        qs = [rope_fn(qa_ref[:, (g * A_GROUP + j) * HD:(g * A_GROUP + j + 1) * HD], q0, tq)
              for j in range(A_GROUP)]
        q = jnp.concatenate(qs, axis=0).astype(BF16)
        ks = ka_s[pl.ds(start, wlen), g * HD:(g + 1) * HD]
        vs = va_s[pl.ds(start, wlen), g * HD:(g + 1) * HD]
        ls = _dot_nt(q, ks) * scale
        if window:
            rows = lax.broadcasted_iota(jnp.int32, ls.shape, 0)
            cols = lax.broadcasted_iota(jnp.int32, ls.shape, 1)
            qpos = q0 + rows % tq
            kpos = start + cols
            ls = jnp.where(jnp.abs(kpos - qpos) <= WINDOW, ls, NEG)
        sink_col = jnp.concatenate(
            [jnp.full((tq, 1), sink_ref[g, j], F32) for j in range(A_GROUP)], axis=0)
        m = jnp.maximum(jnp.max(ls, axis=-1, keepdims=True), sink_col)
        if n_cache:
            lc = _dot_nt(q, cka_s[:, g * HD:(g + 1) * HD]) * scale
            m = jnp.maximum(m, jnp.max(lc, axis=-1, keepdims=True))
        ps = jnp.exp(ls - m)
        den = jnp.sum(ps, axis=-1, keepdims=True) + jnp.exp(sink_col - m)
        o = _dot(ps.astype(BF16), vs)
        if n_cache:
            pc = jnp.exp(lc - m)
            den = den + jnp.sum(pc, axis=-1, keepdims=True)
            o = o + _dot(pc.astype(BF16), cva_s[:, g * HD:(g + 1) * HD])
        o = o / den
        for j in range(A_GROUP):
            hcol = (g * A_GROUP + j) * HD
            o_ref[:, hcol:hcol + HD] = o[j * tq:(j + 1) * tq].astype(BF16)

    lam = (jnp.exp(jnp.sum(lam_ref[0:1, :] * lam_ref[1:2, :], axis=-1, keepdims=True))
           - jnp.exp(jnp.sum(lam_ref[2:3, :] * lam_ref[3:4, :], axis=-1, keepdims=True)) + lam_init)
    for h in range(B_HEADS):
        outs = []
        for w in range(2):
            col = (2 * h + w) * HD
            q = rope_fn(qb_ref[:, col:col + HD], q0, tq).astype(BF16)
            ls = _dot_nt(q, kb_s[:, col:col + HD]) * scale
            m = jnp.max(ls, axis=-1, keepdims=True)
            if n_cache:
                lc = _dot_nt(q, ckb_s[:, col:col + HD]) * scale
                m = jnp.maximum(m, jnp.max(lc, axis=-1, keepdims=True))
            ps = jnp.exp(ls - m)
            den = jnp.sum(ps, axis=-1, keepdims=True)
            o = _dot(ps.astype(BF16), vb_s[:, h * B_VD:(h + 1) * B_VD])
            if n_cache:
                pc = jnp.exp(lc - m)
                den = den + jnp.sum(pc, axis=-1, keepdims=True)
                o = o + _dot(pc.astype(BF16), cvb_s[:, h * B_VD:(h + 1) * B_VD])
            outs.append(o / den)
        ob = outs[0] - lam * outs[1]
        y = ob * lax.rsqrt(jnp.mean(ob * ob, axis=-1, keepdims=True) + EPS)
        y = y * sub_ref[...] * (1.0 - lam_init)
        o_ref[:, A_Q + h * B_VD:A_Q + (h + 1) * B_VD] = y.astype(BF16)


def _attention(p, row0, nb, L, tq, sink, lam_vecs, subln_g, lam_init, name, caches=None,
               rope_tabs=None, emit_kv=False):
    assert row0 % L == 0 and L % tq == 0
    nq = L // tq
    qb0 = row0 // tq
    kb0 = row0 // L
    n_cache = 0 if caches is None else caches[0].shape[1]
    rope = rope_tabs is not None
    in_specs = [
        pl.BlockSpec(memory_space=pltpu.SMEM),
        pl.BlockSpec((4, HD), lambda b, i: (0, 0)),
        pl.BlockSpec((1, B_VD), lambda b, i: (0, 0)),
    ]
    args = [sink, lam_vecs, subln_g.reshape(1, B_VD)]
    if rope:
        in_specs += [pl.BlockSpec((L, HD), lambda b, i: (0, 0))] * 3
        args += list(rope_tabs)
    in_specs += [
        pl.BlockSpec((tq, A_Q), lambda b, i: (qb0 + b * nq + i, 0)),
        pl.BlockSpec((tq, B_QK), lambda b, i: (qb0 + b * nq + i, 1)),
        pl.BlockSpec((L, B_QK), lambda b, i: (kb0 + b, 2)),
        pl.BlockSpec((L, B_V), lambda b, i: (kb0 + b, 3)),
        pl.BlockSpec((L, 2 * A_KV), lambda b, i: (kb0 + b, (A_Q + 2 * B_QK + B_V) // (2 * A_KV))),
    ]
    args += [p] * 5
    scratch = [pltpu.VMEM((L, B_QK), BF16), pltpu.VMEM((L, B_V), BF16),
               pltpu.VMEM((L, A_KV), BF16), pltpu.VMEM((L, A_KV), BF16)]
    if n_cache:
        for cch in caches:
            in_specs.append(pl.BlockSpec((None,) + cch.shape[1:], lambda b, i: (b, 0, 0)))
            scratch.append(pltpu.VMEM(cch.shape[1:], BF16))
        args += list(caches)
    out_specs = [pl.BlockSpec((tq, ATTN_MIX), lambda b, i: (b * nq + i, 0))]
    out_shape = [jax.ShapeDtypeStruct((nb * L, ATTN_MIX), BF16)]
    if emit_kv:
        for width in (A_KV, A_KV, B_QK, B_V):
            out_specs.append(pl.BlockSpec((L, width), lambda b, i: (b, 0)))
            out_shape.append(jax.ShapeDtypeStruct((nb * L, width), F32))
    outs = pl.pallas_call(
        functools.partial(_attn_kernel, L=L, tq=tq, n_cache=n_cache, rope=rope,
                          window=rope, lam_init=lam_init, emit_kv=emit_kv),
        grid=(nb, nq),
        in_specs=in_specs,
        out_specs=out_specs,
        out_shape=out_shape,
        scratch_shapes=scratch,
        compiler_params=_cparams("parallel", "arbitrary"),
        name=name,
    )(*args)
    return outs if emit_kv else outs[0]


def _rope_tables(L):
    rows = L // GRID_W
    row = jnp.repeat(jnp.arange(rows), GRID_W).astype(F32)
    col = jnp.tile(jnp.arange(GRID_W), rows).astype(F32)
    half = HD // 2
    inv = ROPE_BASE ** (-jnp.arange(0, half, 2, dtype=F32) / half)
    ar = row[:, None] * inv
    ac = col[:, None] * inv
    ang = jnp.concatenate([ar, ar, ac, ac], axis=-1)
    cos, sin = jnp.cos(ang), jnp.sin(ang)
    first = (jnp.arange(HD) % (HD // 2)) < (HD // 4)
    return cos, jnp.where(first, -sin, 0.0), jnp.where(first, 0.0, sin)


CONV_ROWS = 256
CONV_HALO = SUBLANES


def _conv_kernel(x_ref, prev_ref, next_ref, w_ref, b_ref, o_ref, pad, *, n_ctx, lp, ld):
    r0 = pl.program_id(0) * CONV_ROWS
    in_seq = jnp.where(r0 < n_ctx, r0 % lp, (r0 - n_ctx) % ld)
    seq_len = jnp.where(r0 < n_ctx, lp, ld)
    has_prev = in_seq > 0
    has_next = in_seq + CONV_ROWS < seq_len
    zeros = jnp.zeros((CONV_HALO, x_ref.shape[1]), F32)
    pad[0:CONV_HALO, :] = jnp.where(has_prev, prev_ref[...], zeros)
    pad[CONV_HALO:CONV_HALO + CONV_ROWS, :] = x_ref[...]
    pad[CONV_HALO + CONV_ROWS:, :] = jnp.where(has_next, next_ref[...], zeros)
    acc = b_ref[...] + x_ref[...] * w_ref[CONV_K // 2:CONV_K // 2 + 1, :]
    for k in range(CONV_K):
        d = k - CONV_K // 2
        if d != 0:
            acc = acc + pad[CONV_HALO + d:CONV_HALO + d + CONV_ROWS, :] * w_ref[k:k + 1, :]
    o_ref[...] = acc * _sigmoid(acc)


def _conv_silu(zx, conv_w, conv_b, n_ctx, lp, ld):
    n = zx.shape[0]
    tc = 1024
    c0 = D_INNER // tc
    hb = CONV_ROWS // CONV_HALO
    nblk = n // CONV_HALO
    return pl.pallas_call(
        functools.partial(_conv_kernel, n_ctx=n_ctx, lp=lp, ld=ld),
        grid=(n // CONV_ROWS, CONV_CH // tc),
        in_specs=[
            pl.BlockSpec((CONV_ROWS, tc), lambda i, j: (i, c0 + j)),
            pl.BlockSpec((CONV_HALO, tc), lambda i, j: (jnp.maximum(i * hb - 1, 0), c0 + j)),
            pl.BlockSpec((CONV_HALO, tc), lambda i, j: (jnp.minimum((i + 1) * hb, nblk - 1), c0 + j)),
            pl.BlockSpec((CONV_K, tc), lambda i, j: (0, j)),
            pl.BlockSpec((1, tc), lambda i, j: (0, j)),
        ],
        out_specs=pl.BlockSpec((CONV_ROWS, tc), lambda i, j: (i, j)),
        out_shape=jax.ShapeDtypeStruct((n, CONV_CH), F32),
        scratch_shapes=[pltpu.VMEM((CONV_ROWS + 2 * CONV_HALO, tc), F32)],
        compiler_params=_cparams("parallel", "parallel"),
        name="ssd_conv",
    )(zx, zx, zx, conv_w, conv_b.reshape(1, CONV_CH))


def _chunk_info(gc, nseq_c, ncc, ncl):
    n_c = nseq_c * ncc
    is_l = gc >= n_c
    seq = jnp.where(is_l, nseq_c + (gc - n_c) // ncl, gc // ncc)
    c = jnp.where(is_l, (gc - n_c) % ncl, gc % ncc)
    nc = jnp.where(is_l, ncl, ncc)
    return seq, c, nc, gc - c, is_l


def _split3(v):
    hi = v.astype(BF16)
    r1 = v - hi.astype(F32)
    mid = r1.astype(BF16)
    lo = (r1 - mid.astype(F32)).astype(BF16)
    return jnp.concatenate([hi, mid, lo], axis=1)


def _ssd_kernel(x_ref, b_ref, c_ref, dt_ref, dtb_ref, alog_ref, h0_ref, e3_ref, y_ref, hfin_ref, h_scr,
                *, nseq_c, ncc, ncl):
    fwd = pl.program_id(0) == 0
    _, c, nc, _, is_l = _chunk_info(pl.program_id(1), nseq_c, ncc, ncl)

    @pl.when(c == 0)
    def _():
        h_scr[...] = jnp.where(is_l, h0_ref[...], jnp.zeros_like(h_scr))

    r = lax.broadcasted_iota(jnp.int32, (CHUNK, CHUNK), 0)
    s = lax.broadcasted_iota(jnp.int32, (CHUNK, CHUNK), 1)
    incl = jnp.where(fwd, r - s, s - r) >= 0
    raw = dt_ref[...] + dtb_ref[...]
    dt = jnp.maximum(raw, 0.0) + jnp.log(1.0 + jnp.exp(-jnp.abs(raw)))
    dta = dt * (-jnp.exp(alog_ref[...]))
    ac = _dot(jnp.where(incl, 1.0, 0.0), dta, precision=HIGHEST)
    ac_last = jnp.where(fwd, ac[CHUNK - 1:CHUNK, :], ac[0:1, :])
    eac = jnp.exp(ac)
    dtd = dt * jnp.exp(ac_last - ac)
    cdec = jnp.exp(ac_last)
    rows_t = jnp.transpose(jnp.concatenate([ac, dt], axis=1))
    lane = lax.broadcasted_iota(jnp.int32, (CHUNK, 2 * SSD_HEADDIM), 1)
    first = lane < SSD_HEADDIM
    pw = 2 * SSD_HEADDIM
    gw = SSD_HPG * SSD_HEADDIM
    dtd3 = _split3(dtd)
    eac3 = _split3(eac)

    for g in range(SSD_GROUPS):
        cols = slice(g * gw, (g + 1) * gw)
        dtd_e = _dot(dtd3, e3_ref[:, cols])
        eac_e = _dot(eac3, e3_ref[:, cols])
        bg = b_ref[:, g * SSD_STATE:(g + 1) * SSD_STATE].astype(BF16)
        cg = c_ref[:, g * SSD_STATE:(g + 1) * SSD_STATE].astype(BF16)
        scores = _dot_nt(cg, bg)
        xg = x_ref[:, cols]
        h_in = h_scr[cols, :]
        y_off = _dot_nt(cg, h_in.astype(BF16)) * eac_e
        st = _dot_tn((xg * dtd_e).astype(BF16), bg)
        yd = []
        for pp in range(SSD_HPG // 2):
            h0 = g * SSD_HPG + 2 * pp
            xp = xg[:, pp * pw:(pp + 1) * pw].astype(BF16)
            two = []
            for h in (h0, h0 + 1):
                decay = jnp.exp(jnp.where(incl, ac[:, h:h + 1] - rows_t[h:h + 1, :], NEG))
                mm = (scores * decay * rows_t[SSD_HEADS + h:SSD_HEADS + h + 1, :]).astype(BF16)
                two.append(_dot(mm, xp))
            yd.append(jnp.where(first, two[0], two[1]))
        y_ref[:, cols] = (jnp.concatenate(yd, axis=1) + y_off).astype(y_ref.dtype)
        for hh in range(SSD_HPG):
            h = g * SSD_HPG + hh
            lo = hh * SSD_HEADDIM
            h_scr[h * SSD_HEADDIM:(h + 1) * SSD_HEADDIM, :] = (
                cdec[:, h:h + 1] * h_in[lo:lo + SSD_HEADDIM, :] + st[lo:lo + SSD_HEADDIM, :])

    @pl.when(jnp.logical_and(c == nc - 1, jnp.logical_not(is_l)))
    def _():
        hfin_ref[...] = h_scr[...]


def _ssd_scan(xc, dt2, dt_bias, a_log, h0, nseq_c, lp, ld):
    n = xc.shape[0]
    ncc, ncl = lp // CHUNK, ld // CHUNK
    nchunks = n // CHUNK
    bd = h0.shape[1]
    sw = SSD_GROUPS * SSD_STATE
    info = functools.partial(_chunk_info, nseq_c=nseq_c, ncc=ncc, ncl=ncl)

    def rowblk(d, gc):
        _, c, nc, start, _ = info(gc)
        return start + jnp.where(d == 0, c, nc - 1 - c)

    def seq_l(gc):
        return jnp.clip(info(gc)[0] - nseq_c, 0, bd - 1)

    def seq_c(gc):
        return jnp.minimum(info(gc)[0], nseq_c - 1)

    head_of_col = jnp.arange(D_INNER) // SSD_HEADDIM
    expand3 = (jnp.arange(3 * SSD_HEADS)[:, None] % SSD_HEADS == head_of_col[None, :]).astype(BF16)

    return pl.pallas_call(
        functools.partial(_ssd_kernel, nseq_c=nseq_c, ncc=ncc, ncl=ncl),
        grid=(2, nchunks),
        in_specs=[
            pl.BlockSpec((CHUNK, D_INNER), lambda d, gc: (rowblk(d, gc), 0)),
            pl.BlockSpec((CHUNK, sw), lambda d, gc: (rowblk(d, gc), D_INNER // sw)),
            pl.BlockSpec((CHUNK, sw), lambda d, gc: (rowblk(d, gc), D_INNER // sw + 1)),
            pl.BlockSpec((None, CHUNK, SSD_HEADS), lambda d, gc: (d, rowblk(d, gc), 0)),
            pl.BlockSpec((None, 1, SSD_HEADS), lambda d, gc: (d, 0, 0)),
            pl.BlockSpec((None, 1, SSD_HEADS), lambda d, gc: (d, 0, 0)),
            pl.BlockSpec((None, None, D_INNER, SSD_STATE), lambda d, gc: (d, seq_l(gc), 0, 0)),
            pl.BlockSpec((3 * SSD_HEADS, D_INNER), lambda d, gc: (0, 0)),
        ],
        out_specs=[
            pl.BlockSpec((None, CHUNK, D_INNER), lambda d, gc: (d, rowblk(d, gc), 0)),
            pl.BlockSpec((None, None, D_INNER, SSD_STATE), lambda d, gc: (d, seq_c(gc), 0, 0)),
        ],
        out_shape=[
            jax.ShapeDtypeStruct((2, n, D_INNER), BF16),
            jax.ShapeDtypeStruct((2, nseq_c, D_INNER, SSD_STATE), F32),
        ],
        scratch_shapes=[pltpu.VMEM((D_INNER, SSD_STATE), F32)],
        compiler_params=_cparams("arbitrary", "arbitrary"),
        name="ssd_scan",
    )(xc, xc, xc, dt2, dt_bias.reshape(2, 1, SSD_HEADS), a_log.reshape(2, 1, SSD_HEADS), h0, expand3)


def _ssd_gate_kernel(yf_ref, yb_ref, xs_ref, z_ref, d_ref, g_ref, o_ref):
    z = z_ref[...]
    y = (yf_ref[...].astype(F32) + yb_ref[...].astype(F32) + d_ref[...] * xs_ref[...]) * (z * _sigmoid(z))
    gs = D_INNER // SSD_GROUPS
    for g in range(SSD_GROUPS):
        yg = y[:, g * gs:(g + 1) * gs]
        n = yg * lax.rsqrt(jnp.mean(yg * yg, axis=-1, keepdims=True) + EPS)
        o_ref[:, g * gs:(g + 1) * gs] = (n * g_ref[:, g * gs:(g + 1) * gs]).astype(BF16)


def _ssd_gate(y2, xc, zx, d_skip, norm_g):
    n = xc.shape[0]
    tm = 128
    d_e = jnp.repeat(d_skip, SSD_HEADDIM).reshape(1, D_INNER)
    return pl.pallas_call(
        _ssd_gate_kernel,
        grid=(n // tm,),
        in_specs=[
            pl.BlockSpec((None, tm, D_INNER), lambda i: (0, i, 0)),
            pl.BlockSpec((None, tm, D_INNER), lambda i: (1, i, 0)),
            pl.BlockSpec((tm, D_INNER), lambda i: (i, 0)),
            pl.BlockSpec((tm, D_INNER), lambda i: (i, 0)),
            pl.BlockSpec((1, D_INNER), lambda i: (0, 0)),
            pl.BlockSpec((1, D_INNER), lambda i: (0, 0)),
        ],
        out_specs=pl.BlockSpec((tm, D_INNER), lambda i: (i, 0)),
        out_shape=jax.ShapeDtypeStruct((n, D_INNER), BF16),
        compiler_params=_cparams("parallel"),
        name="ssd_gate",
    )(y2, y2, xc, zx, d_e, norm_g.reshape(1, D_INNER))


def _oddeven_merge(lo, hi, r):
    step = r * 2
    if step < hi - lo:
        yield from _oddeven_merge(lo, hi, step)
        yield from _oddeven_merge(lo + r, hi, step)
        for i in range(lo + r, hi - r, step):
            yield (i, i + r)
    else:
        yield (lo, lo + r)


def _oddeven_sort(lo, hi):
    if hi - lo >= 1:
        mid = lo + (hi - lo) // 2
        yield from _oddeven_sort(lo, mid)
        yield from _oddeven_sort(mid + 1, hi)
        yield from _oddeven_merge(lo, hi, 1)


def _cmpx(v, i, j):
    a, b = v[i], v[j]
    if b is None:
        return
    if a is None:
        v[i], v[j] = b, None
        return
    v[i], v[j] = jnp.maximum(a, b), jnp.minimum(a, b)


def _top_across_sublanes(v):
    n = len(v)
    for i, j in _oddeven_sort(0, n - 1):
        _cmpx(v, i, j)
    for shift in (4, 2, 1):
        rolled = [None if a is None else pltpu.roll(a, shift, 0) for a in v]
        c = []
        for i in range(n):
            a, b = v[i], rolled[n - 1 - i]
            c.append(a if b is None else (b if a is None else jnp.maximum(a, b)))
        stride = n // 2
        while stride >= 1:
            for i in range(n):
                if i & stride == 0:
                    _cmpx(c, i, i + stride)
            stride //= 2
        v = c
    return v


def _topk_kernel(q_ref, sk_ref, s1_ref, s2_ref, aux_ref, s_scr, sv_scr, aux_scr):
    tk = q_ref.shape[0]
    ng = tk // LANES
    nhc = 2 * PEER_HEADS
    for hc in range(nhc):
        qh = q_ref[:, hc * PEER_DHALF:(hc + 1) * PEER_DHALF].astype(BF16)
        s_t = _dot_nt(sk_ref[hc].astype(BF16), qh)
        if hc % 2 == 0:
            s1_ref[hc // 2] = s_t
        else:
            s2_ref[hc // 2] = s_t
        for g in range(ng):
            s_scr[g * nhc + hc] = s_t[:, g * LANES:(g + 1) * LANES]

    def stage1(k, carry):
        x = s_scr[k]
        v = _top_across_sublanes([x[SUBLANES * i:SUBLANES * (i + 1), :] for i in range(PEER_TOPK)])
        sv_scr[k] = jnp.concatenate([a[0:1, :] for a in v], axis=0)
        return carry

    lax.fori_loop(0, ng * nhc, stage1, 0)

    def stage2(k, carry):
        g = k // PEER_HEADS
        h = k % PEER_HEADS
        sv1 = sv_scr[g * nhc + 2 * h]
        sv2 = sv_scr[g * nhc + 2 * h + 1]
        b1 = [jnp.broadcast_to(sv1[i:i + 1, :], (SUBLANES, LANES)) for i in range(SUBLANES)]
        b20 = jnp.broadcast_to(sv2[0:1, :], (SUBLANES, LANES))
        lo2, hi2, hi1 = sv2[0:SUBLANES, :], sv2[SUBLANES:, :], sv1[SUBLANES:, :]
        cands = [b1[0] + lo2, b1[0] + hi2] + [b1[i] + lo2 for i in range(1, SUBLANES)] + [hi1 + b20]
        t = _top_across_sublanes(cands + [None] * (PEER_TOPK - len(cands)))
        z = jnp.zeros_like(t[0])
        for a in t:
            z = z + jnp.exp(a - t[0])
        aux_scr[k] = jnp.concatenate(
            [t[PEER_TOPK - 1][0:1, :],
             b1[0][0:1, :],
             b20[0:1, :],
             z[0:1, :]], axis=0)
        return carry

    lax.fori_loop(0, ng * PEER_HEADS, stage2, 0)
    for g in range(ng):
        for h in range(PEER_HEADS):
            aux_ref[h, :, g * LANES:(g + 1) * LANES] = aux_scr[g * PEER_HEADS + h]


def _peer_topk(q, sub_keys):
    n = q.shape[0]
    tk = 2 * LANES
    ng = tk // LANES
    sk = sub_keys.reshape(2 * PEER_HEADS, PEER_KEYS, PEER_DHALF)
    return pl.pallas_call(
        _topk_kernel,
        grid=(n // tk,),
        in_specs=[
            pl.BlockSpec((tk, PEER_HEADS * PEER_DKEY), lambda i: (i, 0)),
            pl.BlockSpec(sk.shape, lambda i: (0, 0, 0)),
        ],
        out_specs=[
            pl.BlockSpec((PEER_HEADS, PEER_KEYS, tk), lambda i: (0, 0, i)),
            pl.BlockSpec((PEER_HEADS, PEER_KEYS, tk), lambda i: (0, 0, i)),
            pl.BlockSpec((PEER_HEADS, 4, tk), lambda i: (0, 0, i)),
        ],
        out_shape=[
            jax.ShapeDtypeStruct((PEER_HEADS, PEER_KEYS, n), F32),
            jax.ShapeDtypeStruct((PEER_HEADS, PEER_KEYS, n), F32),
            jax.ShapeDtypeStruct((PEER_HEADS, 4, n), F32),
        ],
        scratch_shapes=[
            pltpu.VMEM((ng * 2 * PEER_HEADS, PEER_KEYS, LANES), F32),
            pltpu.VMEM((ng * 2 * PEER_HEADS, PEER_TOPK, LANES), F32),
            pltpu.VMEM((ng * PEER_HEADS, 4, LANES), F32),
        ],
        compiler_params=_cparams("parallel"),
        name="peer_topk",
    )(q, sk)


_GELU_K0 = -2.0 * math.sqrt(2.0 / math.pi) * math.log2(math.e)
_GELU_K1 = 0.044715 * _GELU_K0


def _gelu_tanh(x):
    return x / (1.0 + jnp.exp2(x * (_GELU_K0 + _GELU_K1 * (x * x))))


PEER_ROWS = 64


def _peer_kernel(hb_ref, u_ref, v_ref, s1_ref, s2_ref, aux_ref, x_ref, gate_ref, o_ref,
                 acc, st, act, a2, m1z):
    e = pl.program_id(1)
    te, tm = st.shape

    @pl.when(e == 0)
    def _():
        acc[...] = jnp.zeros_like(acc)
        for h in range(PEER_HEADS):
            a2[h] = jnp.exp(s2_ref[h] - aux_ref[h, 2:3, :])
            m1z[h:h + 1, :] = aux_ref[h, 1:2, :] + jnp.log(aux_ref[h, 3:4, :])

    st[...] = _dot_nt(u_ref[...], hb_ref[...])
    for a in range(te // PEER_KEYS):
        for tc in range(tm // LANES):
            ln = slice(tc * LANES, (tc + 1) * LANES)
            s1row = [s1_ref[h, a:a + 1, ln] for h in range(PEER_HEADS)]
            a1row = [jnp.exp(s1row[h] - m1z[h:h + 1, ln]) for h in range(PEER_HEADS)]
            for k0 in range(0, PEER_KEYS, PEER_ROWS):
                kr = slice(k0, k0 + PEER_ROWS)
                w = None
                for h in range(PEER_HEADS):
                    sel = (s1row[h] + s2_ref[h, kr, ln]) >= aux_ref[h, 0:1, ln]
                    t = jnp.where(sel, a2[h, kr, ln], 0.0) * a1row[h]
                    w = t if w is None else w + t
                rows = slice(a * PEER_KEYS + k0, a * PEER_KEYS + k0 + PEER_ROWS)
                act[rows, ln] = (_gelu_tanh(st[rows, ln]) * w).astype(BF16)
    acc[...] += _dot_tn(act[...], v_ref[...])

    @pl.when(e == pl.num_programs(1) - 1)
    def _():
        o_ref[...] = x_ref[...] + gate_ref[0] * acc[...]


def _peer_dense(hb, u_b, v_b, s1, s2, aux, x, mod, gate_k, n_ctx, ld, tm=512, te=1024):
    n, d = x.shape
    ne = PEER_EXPERTS // te
    cond = functools.partial(_cond_of_tile, tm=tm, n_ctx=n_ctx, ld=ld)
    once = dict(pipeline_mode=pl.Buffered(1))
    return pl.pallas_call(
        _peer_kernel,
        grid=(n // tm, ne),
        in_specs=[
            pl.BlockSpec((tm, d), lambda i, e: (i, 0), **once),
            pl.BlockSpec((te, d), lambda i, e: (e, 0)),
            pl.BlockSpec((te, d), lambda i, e: (e, 0)),
            pl.BlockSpec((PEER_HEADS, te // PEER_KEYS, tm), lambda i, e: (0, e, i)),
            pl.BlockSpec((PEER_HEADS, PEER_KEYS, tm), lambda i, e: (0, 0, i), **once),
            pl.BlockSpec((PEER_HEADS, 4, tm), lambda i, e: (0, 0, i)),
            pl.BlockSpec((tm, d), lambda i, e: (i, 0), **once),
            pl.BlockSpec((1, 1, d), lambda i, e: (cond(i) * 6 + gate_k, 0, 0)),
        ],
        out_specs=pl.BlockSpec((tm, d), lambda i, e: (i, 0)),
        out_shape=jax.ShapeDtypeStruct((n, d), F32),
        scratch_shapes=[
            pltpu.VMEM((tm, d), F32),
            pltpu.VMEM((te, tm), F32),
            pltpu.VMEM((te, tm), BF16),
            pltpu.VMEM((PEER_HEADS, PEER_KEYS, tm), F32),
            pltpu.VMEM((PEER_HEADS, tm), F32),
        ],
        compiler_params=_cparams("parallel", "arbitrary"),
        name="peer_dense",
    )(hb, u_b, v_b, s1, s2, aux, x, mod)


def _peer_layer(x, mod, norm_g, w_q, sub_keys, u, v, n_ctx, ld, tm):
    q, hb = _proj(x, w_q.astype(BF16), tm=tm, tn=w_q.shape[1], n_ctx=n_ctx, ld=ld, name="peer_q", mod=mod,
                  norm_g=norm_g, shift_k=3, scale_k=4, emit_h=True)
    s1, s2, aux = _peer_topk(q, sub_keys)
    return _peer_dense(hb, u.astype(BF16), v.astype(BF16), s1, s2, aux, x, mod, 5, n_ctx, ld)


def _rmsnorm_kernel(x_ref, g_ref, oc_ref, ol_ref, *, ctx_tiles):
    x = x_ref[...]
    y = x * lax.rsqrt(jnp.mean(x * x, axis=-1, keepdims=True) + EPS) * g_ref[...]
    i = pl.program_id(0)

    @pl.when(i < ctx_tiles)
    def _():
        oc_ref[...] = y

    @pl.when(i >= ctx_tiles)
    def _():
        ol_ref[...] = y


def _final_norm(x, g, n_ctx):
    n, d = x.shape
    tm = 512
    ctx_tiles = n_ctx // tm
    return pl.pallas_call(
        functools.partial(_rmsnorm_kernel, ctx_tiles=ctx_tiles),
        grid=(n // tm,),
        in_specs=[pl.BlockSpec((tm, d), lambda i: (i, 0)), pl.BlockSpec((1, d), lambda i: (0, 0))],
        out_specs=[pl.BlockSpec((tm, d), lambda i: (jnp.minimum(i, ctx_tiles - 1), 0)),
                   pl.BlockSpec((tm, d), lambda i: (jnp.maximum(i - ctx_tiles, 0), 0))],
        out_shape=[jax.ShapeDtypeStruct((n_ctx, d), F32), jax.ShapeDtypeStruct((n - n_ctx, d), F32)],
        compiler_params=_cparams("arbitrary"),
        name="final_norm",
    )(x, g.reshape(1, d))


def _diff_lambda_init(layer):
    return 0.8 - 0.6 * math.exp(-0.3 * layer)


def kernel(x_prompt, x_sample, cache_a_k, cache_a_v, cache_b_k, cache_b_v, state_ssd_fwd, state_ssd_bwd, c, c_ctx, w_mod, b_mod, norm_g, attn_w_in, attn_w_out, attn_sink, diff_lam, diff_subln_g, ssd_w_in, ssd_conv_w, ssd_conv_b, ssd_dt_bias, ssd_a_log, ssd_d, ssd_norm_g, ssd_w_out, peer_w_q, peer_sub_keys, peer_u, peer_v, final_g):
    bp, lp, d = x_prompt.shape
    bd, ld, _ = x_sample.shape
    past = cache_a_k.shape[2]
    n_ctx, n_lat = bp * lp, bd * ld
    tm = 512
    assert n_ctx % ld == 0 and n_ctx % tm == 0 and ld % tm == 0
    assert lp % CONV_ROWS == 0 and ld % CONV_ROWS == 0

    x = (x_prompt.reshape(n_ctx, d), x_sample.reshape(n_lat, d))

    r = -(-(bd + 1) // SUBLANES) * SUBLANES
    cond = jnp.concatenate([c_ctx[None], c, jnp.zeros((r - bd - 1, d), F32)], axis=0)
    mod_all = _modulation(cond, w_mod, b_mod).reshape(DEPTH, r * 6, 1, d)

    outs = {}
    for l in range(DEPTH):
        i = l // 2
        mod = mod_all[l]
        if l % 2 == 0:
            lam_init = _diff_lambda_init(l)
            w_in = attn_w_in[i]
            sp = (0, A_Q, A_Q + A_KV, A_Q + 2 * A_KV, A_Q + 2 * A_KV + B_QK, A_Q + 2 * A_KV + 2 * B_QK, ATTN_IN)
            aq, ak, av, bq, bk, bv = (w_in[:, sp[k]:sp[k + 1]] for k in range(6))
            w_perm = jnp.concatenate([aq, bq, bk, bv, ak, av], axis=1).astype(BF16)
            p = _proj(x, w_perm, tm=tm, tn=1152, n_ctx=n_ctx, ld=ld, name="attn_in", mod=mod,
                      norm_g=norm_g[l, 0], shift_k=0, scale_k=1)
            mix_c, ak_c, av_c, bk_c, bv_c = _attention(
                p, 0, bp, lp, lp, attn_sink[i], diff_lam[i], diff_subln_g[i], lam_init, "attn_ctx",
                emit_kv=True)
            caches = (cache_a_k[:, i].reshape(bd, past, A_KV), cache_a_v[:, i].reshape(bd, past, A_KV),
                      cache_b_k[:, i].reshape(bd, past, B_QK), cache_b_v[:, i].reshape(bd, past, B_V))
            mix_l = _attention(p, n_ctx, bd, ld, WINDOW, attn_sink[i], diff_lam[i], diff_subln_g[i],
                               lam_init, "attn_lat", caches=caches, rope_tabs=_rope_tables(ld))
            x = _proj((mix_c, mix_l), attn_w_out[i].astype(BF16), tm=tm, tn=d, n_ctx=n_ctx, ld=ld,
                      name="attn_out", mod=mod, res=x, gate_k=2)
            outs.setdefault("ak", []).append(ak_c.reshape(bp, lp, A_KV_HEADS, HD))
            outs.setdefault("av", []).append(av_c.reshape(bp, lp, A_KV_HEADS, HD))
            outs.setdefault("bk", []).append(bk_c.reshape(bp, lp, B_HEADS, 2, HD))
            outs.setdefault("bv", []).append(bv_c.reshape(bp, lp, B_HEADS, B_VD))
        else:
            w_in = ssd_w_in[i]
            nzx = D_INNER + CONV_CH
            zx = _proj(x, w_in.astype(BF16), tm=tm, tn=(nzx + 2 * SSD_HEADS) // 9, n_ctx=n_ctx, ld=ld,
                       name="ssd_in", mod=mod, norm_g=norm_g[l, 0], shift_k=0, scale_k=1)
            dtr = zx[:, nzx:]
            dt2 = jnp.transpose(dtr.reshape(-1, 2, SSD_HEADS), (1, 0, 2))
            xc = _conv_silu(zx, ssd_conv_w[i], ssd_conv_b[i], n_ctx, lp, ld)
            h0 = jnp.stack([state_ssd_fwd[:, i], state_ssd_bwd[:, i]]).reshape(2, bd, D_INNER, SSD_STATE)
            y2, hfin = _ssd_scan(xc, dt2, ssd_dt_bias[i], ssd_a_log[i], h0, bp, lp, ld)
            hg = _ssd_gate(y2, xc, zx, ssd_d[i], ssd_norm_g[i])
            x = _proj(hg, ssd_w_out[i].astype(BF16), tm=tm, tn=d, n_ctx=n_ctx, ld=ld, name="ssd_out",
                      mod=mod, res=x, gate_k=2)
            hs = hfin.reshape(2, bp, SSD_HEADS, SSD_HEADDIM, SSD_STATE)
            outs.setdefault("sf", []).append(hs[0])
            outs.setdefault("sb", []).append(hs[1])
        x = _peer_layer(x, mod, norm_g[l, 1], peer_w_q[l], peer_sub_keys[l], peer_u[l], peer_v[l],
                        n_ctx, ld, tm)

    y_c, y_l = _final_norm(x, final_g, n_ctx)
    y_prompt = y_c.reshape(bp, lp, d)
    y_sample = y_l.reshape(bd, ld, d)
    return (y_prompt, y_sample, jnp.stack(outs["ak"], axis=1), jnp.stack(outs["av"], axis=1),
            jnp.stack(outs["bk"], axis=1), jnp.stack(outs["bv"], axis=1),
            jnp.stack(outs["sf"], axis=1), jnp.stack(outs["sb"], axis=1))
```
